```python
import jax, jax.numpy as jnp
from jax import lax
import numpy as np

D_MODEL = 2048
BATCH = 8
SEQ = 4096
DEPTH = 4

N_HEADS = 16
HEAD_DIM = D_MODEL // N_HEADS
ATTN_GROUPS = ((128, 1), (512, 4), (2048, 16))
N_GROUPS = len(ATTN_GROUPS)
Q_BLOCK = 128
D_RNN = D_MODEL
RNN_BLOCK = 256
N_RNN_BLOCKS = D_RNN // RNN_BLOCK
CONV_WIDTH = 4
LRU_C = 8.0
D_FF = ((8 * D_MODEL // 3 + 511) // 512) * 512
N_EXPERTS = 8
TOP_K = 2
EXPERT_BLOCK = 256
ADA_SCALE = 0.1
LN_EPS = 1e-5
NEG_INF = -1e30
DEEPNORM_ALPHA = (2 * DEPTH) ** 0.25
DEEPNORM_BETA = (8 * DEPTH) ** -0.25
N_ATTN_LAYERS = (DEPTH + 1) // 2
N_RNN_LAYERS = DEPTH // 2

kernel_name = 'hybrid_dilated_attn_rglru_moe_trunk'


def alibi_slopes():
    return 2.0 ** (-8.0 * jnp.arange(1, N_HEADS + 1, dtype=jnp.float32) / N_HEADS)


def layer_norm(x, g, b):
    xf = x.astype(jnp.float32)
    mu = xf.mean(-1, keepdims=True)
    var = jnp.square(xf - mu).mean(-1, keepdims=True)
    y = (xf - mu) * lax.rsqrt(var + LN_EPS) * g.astype(jnp.float32) + b.astype(jnp.float32)
    return y.astype(x.dtype)


def dilated_window_attention(q, k, v, steps, dilation, slopes):
    B, S, H, Dh = q.shape
    L = S // dilation
    nb = -(-L // Q_BLOCK)
    Lp = nb * Q_BLOCK

    def to_blocks(t):
        t = t.reshape(B, L, dilation, H, Dh).transpose(0, 2, 1, 3, 4)
        t = jnp.pad(t, ((0, 0), (0, 0), (0, Lp - L), (0, 0), (0, 0)))
        return t.reshape(B, dilation, nb, Q_BLOCK, H, Dh).astype(jnp.float32)

    def with_prev(t):
        prev = jnp.pad(t, ((0, 0), (0, 0), (1, 0), (0, 0), (0, 0), (0, 0)))[:, :, :-1]
        return jnp.concatenate([prev, t], axis=3)

    qb = to_blocks(q)
    kk = with_prev(to_blocks(k))
    vv = with_prev(to_blocks(v))
    s = jnp.einsum('brnqhe,brnkhe->brnhqk', qb, kk) * (Dh ** -0.5)
    qi = jnp.arange(Q_BLOCK)[:, None]
    ki = jnp.arange(2 * Q_BLOCK)[None, :]
    rel = qi + Q_BLOCK - ki
    key_pos = jnp.arange(nb)[:, None, None] * Q_BLOCK - Q_BLOCK + ki[None]
    valid = (rel >= 0)[None] & (rel <= steps)[None] & (key_pos >= 0)
    alibi = -slopes[:, None, None] * (rel * dilation).astype(jnp.float32)[None]
    s = jnp.where(valid[:, None], s + alibi, NEG_INF)
    m = s.max(-1)
    p = jnp.exp(s - m[..., None])
    l = p.sum(-1)
    o = jnp.einsum('brnhqk,brnkhe->brnqhe', p, vv) / jnp.swapaxes(l, -1, -2)[..., None]
    lse = jnp.swapaxes(m + jnp.log(l), -1, -2)
    o = o.reshape(B, dilation, Lp, H, Dh)[:, :, :L].transpose(0, 2, 1, 3, 4).reshape(B, S, H, Dh)
    lse = lse.reshape(B, dilation, Lp, H)[:, :, :L].transpose(0, 2, 1, 3).reshape(B, S, H)
    return o, lse


def attention_mixer(u, w_qkv, w_o):
    B, S, _ = u.shape
    qkv = (u @ w_qkv).reshape(B, S, N_GROUPS, 3, N_HEADS, HEAD_DIM)
    slopes = alibi_slopes()
    outs, lses = [], []
    for g, (window, dilation) in enumerate(ATTN_GROUPS):
        o, lse = dilated_window_attention(qkv[:, :, g, 0], qkv[:, :, g, 1], qkv[:, :, g, 2],
                                          window // dilation, dilation, slopes)
        outs.append(o)
        lses.append(lse)
    wts = jax.nn.softmax(jnp.stack(lses, axis=0), axis=0)
    o = outs[0] * wts[0][..., None]
    for g in range(1, N_GROUPS):
        o = o + outs[g] * wts[g][..., None]
    return o.reshape(B, S, N_HEADS * HEAD_DIM).astype(u.dtype) @ w_o


def _lru_combine(left, right):
    a_l, b_l = left
    a_r, b_r = right
    return a_l * a_r, a_r * b_l + b_r


def rglru_mixer(u, w_in, conv_w, conv_b, ga_w, ga_b, gx_w, gx_b, lam, w_out):
    B, S, _ = u.shape
    gate_br, rec = jnp.split(u @ w_in, 2, axis=-1)
    xc = lax.conv_general_dilated(rec, conv_w[:, None, :], window_strides=(1,),
                                  padding=((CONV_WIDTH - 1, 0),),
                                  dimension_numbers=('NWC', 'WIO', 'NWC'),
                                  feature_group_count=D_RNN) + conv_b
    xblk = xc.reshape(B, S, N_RNN_BLOCKS, RNN_BLOCK)
    r = jax.nn.sigmoid(jnp.einsum('bsni,nio->bsno', xblk, ga_w).reshape(B, S, D_RNN).astype(jnp.float32)
                       + ga_b.astype(jnp.float32))
    i = jax.nn.sigmoid(jnp.einsum('bsni,nio->bsno', xblk, gx_w).reshape(B, S, D_RNN).astype(jnp.float32)
                       + gx_b.astype(jnp.float32))
    log_a = -LRU_C * r * jax.nn.softplus(-lam.astype(jnp.float32))
    a = jnp.exp(log_a)
    b = jnp.sqrt(-jnp.expm1(2.0 * log_a)) * (i * xc.astype(jnp.float32))
    _, h = lax.associative_scan(_lru_combine, (a, b), axis=1)
    y = (jax.nn.gelu(gate_br.astype(jnp.float32)) * h).astype(u.dtype)
    return y @ w_out


def dense_swiglu(u, w_in, w_out):
    g, up = jnp.split(u @ w_in, 2, axis=-1)
    return (jax.nn.silu(g) * up) @ w_out


def moe_swiglu(u, w_router, w_in, w_out):
    B, S, D = u.shape
    T = B * S
    A = T * TOP_K
    xt = u.reshape(T, D)
    logits = (xt @ w_router).astype(jnp.float32)
    top_logits, top_idx = lax.top_k(logits, TOP_K)
    top_w = jax.nn.softmax(top_logits, axis=-1)
    e_flat = top_idx.reshape(-1)
    w_flat = top_w.reshape(-1)
    tok_flat = jnp.arange(A, dtype=jnp.int32) // TOP_K
    order = jnp.argsort(e_flat)
    e_s, tok_s, w_s = e_flat[order], tok_flat[order], w_flat[order]
    counts = jnp.bincount(e_flat, length=N_EXPERTS)
    padded = (counts + EXPERT_BLOCK - 1) // EXPERT_BLOCK * EXPERT_BLOCK
    pad_end = jnp.cumsum(padded)
    pad_start = pad_end - padded
    cnt_start = jnp.cumsum(counts) - counts
    dest = pad_start[e_s] + (jnp.arange(A) - cnt_start[e_s])
    P = A + N_EXPERTS * EXPERT_BLOCK
    n_blk = P // EXPERT_BLOCK
    buf_tok = jnp.full((P,), T, dtype=jnp.int32).at[dest].set(tok_s)
    buf_w = jnp.zeros((P,), jnp.float32).at[dest].set(w_s)
    blk_expert = jnp.minimum(jnp.searchsorted(pad_end, jnp.arange(n_blk) * EXPERT_BLOCK, side='right'),
                             N_EXPERTS - 1)
    x_pad = jnp.concatenate([xt, jnp.zeros((1, D), xt.dtype)], axis=0)
    xb = x_pad[buf_tok].reshape(n_blk, EXPERT_BLOCK, D)

    def expert_block(args):
        xblk, e = args
        g, up = jnp.split(xblk @ w_in[e], 2, axis=-1)
        return (jax.nn.silu(g) * up) @ w_out[e]

    yb = lax.map(expert_block, (xb, blk_expert)).reshape(P, D)
    y = jnp.zeros((T + 1, D), jnp.float32).at[buf_tok].add(yb.astype(jnp.float32) * buf_w[:, None])[:T]
    return y.astype(u.dtype).reshape(B, S, D)


def setup_inputs(seed: int = 0) -> dict:
    key = jax.random.key(seed)
    ks = jax.random.split(key, 22)
    D = D_MODEL
    nrm = jax.random.normal
    beta = DEEPNORM_BETA
    qkv_scale = jnp.array([1.0, 1.0, beta], dtype=jnp.float32).reshape(1, 1, 1, 3, 1)
    a0 = jax.random.uniform(ks[15], (N_RNN_LAYERS, D_RNN), minval=0.9, maxval=0.999)
    return {
        'x': nrm(ks[0], (BATCH, SEQ, D), jnp.float32),
        'c': nrm(ks[1], (BATCH, D), jnp.float32),
        'ada_w': nrm(ks[2], (DEPTH, D, 6 * D), jnp.float32) * (ADA_SCALE * D ** -0.5),
        'ada_b': 0.01 * nrm(ks[3], (DEPTH, 6 * D), jnp.float32),
        'ln_g': 1.0 + 0.02 * nrm(ks[4], (DEPTH, 2, D), jnp.float32),
        'ln_b': 0.02 * nrm(ks[5], (DEPTH, 2, D), jnp.float32),
        'attn_w_qkv': (nrm(ks[6], (N_ATTN_LAYERS, D, N_GROUPS, 3, N_HEADS * HEAD_DIM), jnp.float32)
                       * (D ** -0.5) * qkv_scale).reshape(N_ATTN_LAYERS, D, N_GROUPS * 3 * N_HEADS * HEAD_DIM),
        'attn_w_o': nrm(ks[7], (N_ATTN_LAYERS, N_HEADS * HEAD_DIM, D), jnp.float32) * ((N_HEADS * HEAD_DIM) ** -0.5 * beta),
        'rg_w_in': nrm(ks[8], (N_RNN_LAYERS, D, 2 * D_RNN), jnp.float32) * D ** -0.5,
        'rg_conv_w': nrm(ks[9], (N_RNN_LAYERS, CONV_WIDTH, D_RNN), jnp.float32) * CONV_WIDTH ** -0.5,
        'rg_conv_b': 0.01 * nrm(ks[10], (N_RNN_LAYERS, D_RNN), jnp.float32),
        'rg_gate_a_w': nrm(ks[11], (N_RNN_LAYERS, N_RNN_BLOCKS, RNN_BLOCK, RNN_BLOCK), jnp.float32) * RNN_BLOCK ** -0.5,
        'rg_gate_a_b': 0.01 * nrm(ks[12], (N_RNN_LAYERS, D_RNN), jnp.float32),
        'rg_gate_x_w': nrm(ks[13], (N_RNN_LAYERS, N_RNN_BLOCKS, RNN_BLOCK, RNN_BLOCK), jnp.float32) * RNN_BLOCK ** -0.5,
        'rg_gate_x_b': 0.01 * nrm(ks[14], (N_RNN_LAYERS, D_RNN), jnp.float32),
        'rg_lambda': jnp.log(a0) - jnp.log1p(-a0),
        'rg_w_out': nrm(ks[16], (N_RNN_LAYERS, D_RNN, D), jnp.float32) * (D_RNN ** -0.5 * beta),
        'ffn_w_in': nrm(ks[17], (N_ATTN_LAYERS, D, 2 * D_FF), jnp.float32) * D ** -0.5,
        'ffn_w_out': nrm(ks[18], (N_ATTN_LAYERS, D_FF, D), jnp.float32) * (D_FF ** -0.5 * beta),
        'moe_w_router': nrm(ks[19], (N_RNN_LAYERS, D, N_EXPERTS), jnp.float32) * D ** -0.5,
        'moe_w_in': nrm(ks[20], (N_RNN_LAYERS, N_EXPERTS, D, 2 * D_FF), jnp.float32) * D ** -0.5,
        'moe_w_out': nrm(ks[21], (N_RNN_LAYERS, N_EXPERTS, D_FF, D), jnp.float32) * (D_FF ** -0.5 * beta),
    }


def reference(x, c, ada_w, ada_b, ln_g, ln_b, attn_w_qkv, attn_w_o, rg_w_in, rg_conv_w, rg_conv_b,
              rg_gate_a_w, rg_gate_a_b, rg_gate_x_w, rg_gate_x_b, rg_lambda, rg_w_out,
              ffn_w_in, ffn_w_out, moe_w_router, moe_w_in, moe_w_out):
    c_act = jax.nn.silu(c)
    for i in range(DEPTH):
        j = i // 2
        mod = c_act @ ada_w[i] + ada_b[i]
        sh1, sc1, g1, sh2, sc2, g2 = jnp.split(mod[:, None, :], 6, axis=-1)
        u = x * (1.0 + sc1) + sh1
        if i % 2 == 0:
            y = attention_mixer(u, attn_w_qkv[j], attn_w_o[j])
        else:
            y = rglru_mixer(u, rg_w_in[j], rg_conv_w[j], rg_conv_b[j], rg_gate_a_w[j], rg_gate_a_b[j],
                            rg_gate_x_w[j], rg_gate_x_b[j], rg_lambda[j], rg_w_out[j])
        x = layer_norm(DEEPNORM_ALPHA * x + (1.0 + g1) * y, ln_g[i, 0], ln_b[i, 0])
        u = x * (1.0 + sc2) + sh2
        if i % 2 == 0:
            y = dense_swiglu(u, ffn_w_in[j], ffn_w_out[j])
        else:
            y = moe_swiglu(u, moe_w_router[j], moe_w_in[j], moe_w_out[j])
        x = layer_norm(DEEPNORM_ALPHA * x + (1.0 + g2) * y, ln_g[i, 1], ln_b[i, 1])
    return x
```

```python
import functools

import jax
import jax.numpy as jnp
from jax import lax
from jax.experimental import pallas as pl
from jax.experimental.pallas import tpu as pltpu

F32 = jnp.float32
BF16 = jnp.bfloat16

N_HEADS = 16
ATTN_GROUPS = ((128, 1), (512, 4), (2048, 16))
Q_BLOCK = 128
RNN_BLOCK = 256
CONV_WIDTH = 4
LRU_C = 8.0
N_EXPERTS = 8
TOP_K = 2
LN_EPS = 1e-5
NEG_INF = -1e30
N_MOD = 6

LANES = 128
SUBLANES = 8
VMEM_BYTES = 64 * 1024 * 1024

MOE_TILE_ROWS = 512
DMA_ROWS = 256


def _vmem_limit(block_bytes):
    want = 2 * block_bytes + 12 * 1024 * 1024
    return int(min(want, VMEM_BYTES - 6 * 1024 * 1024))


def _params(semantics, block_bytes):
    return pltpu.CompilerParams(dimension_semantics=semantics,
                                vmem_limit_bytes=_vmem_limit(block_bytes))


def _nbytes(shape, dtype):
    n = 1
    for s in shape:
        n *= s
    return n * jnp.dtype(dtype).itemsize


def _ada_kernel(c_ref, w_ref, b_ref, o_ref):
    c = c_ref[...]
    c_act = (c * jax.nn.sigmoid(c)).astype(BF16)
    o_ref[...] = jnp.dot(c_act, w_ref[...].astype(BF16), preferred_element_type=F32) + b_ref[...]


def ada_modulation(c, ada_w, ada_b):
    depth, d, n = ada_w.shape
    b = c.shape[0]
    tn = 1536 if n % 1536 == 0 else n
    blocks = _nbytes((d, tn), F32) + _nbytes((b, tn), F32) * 2
    return pl.pallas_call(
        _ada_kernel,
        grid=(depth, n // tn),
        in_specs=[
            pl.BlockSpec((b, d), lambda i, j: (0, 0)),
            pl.BlockSpec((None, d, tn), lambda i, j: (i, 0, j)),
            pl.BlockSpec((None, 1, tn), lambda i, j: (i, 0, j)),
        ],
        out_specs=pl.BlockSpec((None, b, tn), lambda i, j: (i, 0, j)),
        out_shape=jax.ShapeDtypeStruct((depth, b, n), F32),
        compiler_params=_params(("arbitrary", "arbitrary"), blocks),
        name="ada_modulation",
    )(c, ada_w, ada_b.reshape(depth, 1, n))


def _modulate_kernel(x_ref, sc_ref, sh_ref, u_ref):
    u_ref[...] = (x_ref[...] * (1.0 + sc_ref[...]) + sh_ref[...]).astype(u_ref.dtype)


def _ln_kernel(x_ref, y_ref, gate_ref, lng_ref, lnb_ref, *rest, alpha, with_u):
    if with_u:
        sc_ref, sh_ref, xo_ref, u_ref = rest
    else:
        (xo_ref,) = rest
    z = alpha * x_ref[...] + (1.0 + gate_ref[...]) * y_ref[...].astype(F32)
    mu = jnp.mean(z, axis=-1, keepdims=True)
    zc = z - mu
    var = jnp.mean(zc * zc, axis=-1, keepdims=True)
    xn = zc * lax.rsqrt(var + LN_EPS) * lng_ref[...] + lnb_ref[...]
    xo_ref[...] = xn
    if with_u:
        u_ref[...] = (xn * (1.0 + sc_ref[...]) + sh_ref[...]).astype(u_ref.dtype)


def _row_tile(s):
    for t in (512, 256, 128, 64, 32, 16, 8):
        if s % t == 0:
            return t
    return s


def modulate(x, mod4, sc_chunk, sh_chunk, u_dtype):
    b, s, d = x.shape
    tm = _row_tile(s)
    vec = lambda ch: pl.BlockSpec((None, None, 1, d), lambda bi, ti, ch=ch: (bi, ch, 0, 0))
    row = pl.BlockSpec((None, tm, d), lambda bi, ti: (bi, ti, 0))
    blocks = _nbytes((tm, d), F32) * 2
    return pl.pallas_call(
        _modulate_kernel,
        grid=(b, s // tm),
        in_specs=[row, vec(sc_chunk), vec(sh_chunk)],
        out_specs=row,
        out_shape=jax.ShapeDtypeStruct((b, s, d), u_dtype),
        compiler_params=_params(("arbitrary", "arbitrary"), blocks),
        name="modulate",
    )(x, mod4, mod4)


def residual_layer_norm(x, y, mod4, gate_chunk, ln_g, ln_b, alpha, next_mod4=None, u_dtype=BF16):
    b, s, d = x.shape
    tm = _row_tile(s)
    with_u = next_mod4 is not None
    vec = lambda ch: pl.BlockSpec((None, None, 1, d), lambda bi, ti, ch=ch: (bi, ch, 0, 0))
    row = pl.BlockSpec((None, tm, d), lambda bi, ti: (bi, ti, 0))
    par = pl.BlockSpec((1, d), lambda bi, ti: (0, 0))
    in_specs = [row, row, vec(gate_chunk), par, par]
    args = [x, y.reshape(b, s, d), mod4, ln_g.reshape(1, d), ln_b.reshape(1, d)]
    out_specs = [row]
    out_shape = [jax.ShapeDtypeStruct((b, s, d), F32)]
    n_rows = 3
    if with_u:
        nm4, sc_chunk, sh_chunk = next_mod4
        in_specs += [vec(sc_chunk), vec(sh_chunk)]
        args += [nm4, nm4]
        out_specs.append(row)
        out_shape.append(jax.ShapeDtypeStruct((b, s, d), u_dtype))
        n_rows = 4
    out = pl.pallas_call(
        functools.partial(_ln_kernel, alpha=alpha, with_u=with_u),
        grid=(b, s // tm),
        in_specs=in_specs,
        out_specs=out_specs,
        out_shape=out_shape,
        compiler_params=_params(("arbitrary", "arbitrary"), _nbytes((tm, d), F32) * n_rows),
        name="residual_layer_norm",
    )(*args)
    return (out[0], out[1]) if with_u else (out[0], None)


def _mm_kernel(x_ref, w_ref, o_ref):
    o_ref[...] = jnp.dot(x_ref[...], w_ref[...], preferred_element_type=F32).astype(o_ref.dtype)


def matmul(x, w, out_dtype, tm, tn):
    m, k = x.shape
    n = w.shape[1]
    tm = min(tm, m)
    tn = min(tn, n)
    blocks = _nbytes((tm, k), BF16) + _nbytes((k, tn), BF16) + _nbytes((tm, tn), out_dtype) + _nbytes((tm, tn), F32) // 2
    return pl.pallas_call(
        _mm_kernel,
        grid=(n // tn, m // tm),
        in_specs=[
            pl.BlockSpec((tm, k), lambda j, i: (i, 0)),
            pl.BlockSpec((k, tn), lambda j, i: (0, j)),
        ],
        out_specs=pl.BlockSpec((tm, tn), lambda j, i: (i, j)),
        out_shape=jax.ShapeDtypeStruct((m, n), out_dtype),
        compiler_params=_params(("arbitrary", "arbitrary"), blocks),
        name="matmul",
    )(x, w)


def _swiglu_kernel(x_ref, wg_ref, wu_ref, o_ref):
    x = x_ref[...]
    g = jnp.dot(x, wg_ref[...], preferred_element_type=F32)
    up = jnp.dot(x, wu_ref[...], preferred_element_type=F32)
    o_ref[...] = (g * jax.nn.sigmoid(g) * up).astype(o_ref.dtype)


def _ff_tile(f):
    for t in (512, 256, 128):
        if f % t == 0:
            return t
    return f


def swiglu_in(x, w_in, tm):
    m, k = x.shape
    f = w_in.shape[1] // 2
    tn = _ff_tile(f)
    tm = min(tm, m)
    nf = f // tn
    blocks = _nbytes((tm, k), BF16) + 2 * _nbytes((k, tn), BF16) + _nbytes((tm, tn), BF16) + _nbytes((tm, tn), F32)
    return pl.pallas_call(
        _swiglu_kernel,
        grid=(nf, m // tm),
        in_specs=[
            pl.BlockSpec((tm, k), lambda j, i: (i, 0)),
            pl.BlockSpec((k, tn), lambda j, i: (0, j)),
            pl.BlockSpec((k, tn), lambda j, i: (0, j + nf)),
        ],
        out_specs=pl.BlockSpec((tm, tn), lambda j, i: (i, j)),
        out_shape=jax.ShapeDtypeStruct((m, f), BF16),
        compiler_params=_params(("arbitrary", "arbitrary"), blocks),
        name="swiglu_in",
    )(x, w_in, w_in)


def _attn_kernel(q_ref, kp_ref, kc_ref, vp_ref, vc_ref, o_ref, lse_ref, *, dilation, steps, head_dim):
    n = pl.program_id(2)
    qb = q_ref.shape[0]
    qi = lax.broadcasted_iota(jnp.int32, (qb, qb), 0)
    ki = lax.broadcasted_iota(jnp.int32, (qb, qb), 1)
    rel_prev = qi + qb - ki
    rel_cur = qi - ki
    no_prev = jnp.where(n > 0, 0, 2 * steps + 2 * qb)
    valid_prev = (rel_prev + no_prev) <= steps
    valid_cur = (rel_cur >= 0) & (rel_cur <= steps)
    base_prev = -(rel_prev * dilation).astype(F32)
    base_cur = -(rel_cur * dilation).astype(F32)
    lane = lax.broadcasted_iota(jnp.int32, lse_ref.shape, 1)
    lse_tile = jnp.zeros(lse_ref.shape, F32)
    scale = head_dim ** -0.5
    contract_last = (((1,), (1,)), ((), ()))
    for h in range(N_HEADS):
        slope = 2.0 ** (-8.0 * (h + 1) / N_HEADS)
        hs = slice(h * head_dim, (h + 1) * head_dim)
        q = q_ref[:, hs]
        s_prev = lax.dot_general(q, kp_ref[:, hs], contract_last, preferred_element_type=F32) * scale
        s_cur = lax.dot_general(q, kc_ref[:, hs], contract_last, preferred_element_type=F32) * scale
        s_prev = jnp.where(valid_prev, s_prev + slope * base_prev, NEG_INF)
        s_cur = jnp.where(valid_cur, s_cur + slope * base_cur, NEG_INF)
        m = jnp.maximum(jnp.max(s_prev, axis=-1, keepdims=True), jnp.max(s_cur, axis=-1, keepdims=True))
        p_prev = jnp.exp(s_prev - m)
        p_cur = jnp.exp(s_cur - m)
        l = jnp.sum(p_prev, axis=-1, keepdims=True) + jnp.sum(p_cur, axis=-1, keepdims=True)
        o = (jnp.dot(p_prev.astype(BF16), vp_ref[:, hs], preferred_element_type=F32)
             + jnp.dot(p_cur.astype(BF16), vc_ref[:, hs], preferred_element_type=F32))
        o_ref[:, hs] = (o / l).astype(o_ref.dtype)
        lse_tile = jnp.where(lane == h, m + jnp.log(l), lse_tile)
    lse_ref[...] = lse_tile


def dilated_attention_group(qkv, g, window, dilation):
    b, s, c = qkv.shape
    d = c // (len(ATTN_GROUPS) * 3)
    head_dim = d // N_HEADS
    steps = window // dilation
    assert steps <= Q_BLOCK and s % (dilation * Q_BLOCK) == 0 and N_HEADS <= LANES
    l = s // dilation
    nb = l // Q_BLOCK
    per_tok = c // d
    view = qkv.reshape(b, l, dilation * c)
    col = lambda j: (lambda bi, r, n: (bi, n, r * per_tok + g * 3 + j))
    colp = lambda j: (lambda bi, r, n: (bi, jnp.maximum(n - 1, 0), r * per_tok + g * 3 + j))
    spec = lambda im: pl.BlockSpec((None, Q_BLOCK, d), im)
    blocks = 5 * _nbytes((Q_BLOCK, d), BF16) + _nbytes((Q_BLOCK, d), BF16) + _nbytes((Q_BLOCK, LANES), F32)
    o, lse = pl.pallas_call(
        functools.partial(_attn_kernel, dilation=dilation, steps=steps, head_dim=head_dim),
        grid=(b, dilation, nb),
        in_specs=[spec(col(0)), spec(colp(1)), spec(col(1)), spec(colp(2)), spec(col(2))],
        out_specs=[
            pl.BlockSpec((None, Q_BLOCK, d), lambda bi, r, n: (bi, n, r)),
            pl.BlockSpec((None, Q_BLOCK, LANES), lambda bi, r, n: (bi, n, r)),
        ],
        out_shape=[
            jax.ShapeDtypeStruct((b, l, dilation * d), BF16),
            jax.ShapeDtypeStruct((b, l, dilation * LANES), F32),
        ],
        compiler_params=_params(("arbitrary", "arbitrary", "arbitrary"), blocks),
        name=f"dilated_attention_g{g}",
    )(view, view, view, view, view)
    return o.reshape(b * s, d), lse.reshape(b * s, LANES)


def _merge_kernel(o0_ref, o1_ref, o2_ref, l0_ref, l1_ref, l2_ref, out_ref, *, head_dim):
    l0, l1, l2 = l0_ref[...], l1_ref[...], l2_ref[...]
    m = jnp.maximum(jnp.maximum(l0, l1), l2)
    e0, e1, e2 = jnp.exp(l0 - m), jnp.exp(l1 - m), jnp.exp(l2 - m)
    den = e0 + e1 + e2
    w0, w1, w2 = e0 / den, e1 / den, e2 / den
    for h in range(N_HEADS):
        hs = slice(h * head_dim, (h + 1) * head_dim)
        acc = o0_ref[:, hs].astype(F32) * w0[:, h:h + 1]
        acc = acc + o1_ref[:, hs].astype(F32) * w1[:, h:h + 1]
        acc = acc + o2_ref[:, hs].astype(F32) * w2[:, h:h + 1]
        out_ref[:, hs] = acc.astype(out_ref.dtype)


def merge_groups(outs, lses):
    t, d = outs[0].shape
    tm = _row_tile(t)
    row = pl.BlockSpec((tm, d), lambda i: (i, 0))
    lrow = pl.BlockSpec((tm, LANES), lambda i: (i, 0))
    blocks = 4 * _nbytes((tm, d), BF16) + 3 * _nbytes((tm, LANES), F32) + _nbytes((tm, d), F32)
    return pl.pallas_call(
        functools.partial(_merge_kernel, head_dim=d // N_HEADS),
        grid=(t // tm,),
        in_specs=[row, row, row, lrow, lrow, lrow],
        out_specs=row,
        out_shape=jax.ShapeDtypeStruct((t, d), BF16),
        compiler_params=_params(("arbitrary",), blocks),
        name="merge_groups",
    )(*outs, *lses)


def _softplus(x):
    return jnp.maximum(x, 0.0) + jnp.log1p(jnp.exp(-jnp.abs(x)))


def _rglru_kernel(gate_ref, rec_ref, cw_ref, cb_ref, gaw_ref, gab_ref, gxw_ref, gxb_ref, lam_ref,
                  y_ref, ext_scr, a_scr, b_scr, h_scr, carry_scr):
    t = pl.program_id(1)
    nb, ts, cb = rec_ref.shape
    halo = SUBLANES
    n_half = cb // LANES

    @pl.when(t == 0)
    def _():
        ext_scr[:, 0:halo, :] = jnp.zeros((nb, halo, cb), F32)
        carry_scr[...] = jnp.zeros(carry_scr.shape, F32)

    rec = rec_ref[...]
    ext_scr[:, halo:halo + ts, :] = rec
    xc = cb_ref[...].reshape(1, 1, cb)
    for k in range(CONV_WIDTH):
        off = halo - (CONV_WIDTH - 1) + k
        xc = xc + cw_ref[k:k + 1, :].reshape(1, 1, cb) * ext_scr[:, off:off + ts, :]
    ext_scr[:, 0:halo, :] = rec[:, ts - halo:ts, :]

    xc2 = xc.reshape(nb * ts, cb)
    xb = xc2.astype(BF16)
    r = jax.nn.sigmoid(jnp.dot(xb, gaw_ref[...], preferred_element_type=F32) + gab_ref[...])
    i = jax.nn.sigmoid(jnp.dot(xb, gxw_ref[...], preferred_element_type=F32) + gxb_ref[...])
    log_a = -LRU_C * r * _softplus(-lam_ref[...])
    a = jnp.exp(log_a)
    bb = jnp.sqrt(-jnp.tanh(log_a) * (a * a + 1.0)) * (i * xc2)
    for c in range(n_half):
        a_scr[c] = a[:, c * LANES:(c + 1) * LANES]
        b_scr[c] = bb[:, c * LANES:(c + 1) * LANES]

    def step(j, hs):
        out = []
        for c in range(n_half):
            a_t = a_scr[c, pl.ds(j, nb, stride=ts), :]
            b_t = b_scr[c, pl.ds(j, nb, stride=ts), :]
            h = a_t * hs[c] + b_t
            h_scr[c, pl.ds(j, nb, stride=ts), :] = h
            out.append(h)
        return tuple(out)

    h0 = tuple(carry_scr[c] for c in range(n_half))
    hn = lax.fori_loop(0, ts, step, h0, unroll=8)
    for c in range(n_half):
        carry_scr[c] = hn[c]

    gate = jax.nn.gelu(gate_ref[...]).reshape(nb * ts, cb)
    for c in range(n_half):
        cs = slice(c * LANES, (c + 1) * LANES)
        y_ref[:, :, cs] = (gate[:, cs] * h_scr[c]).reshape(nb, ts, LANES).astype(y_ref.dtype)


def rglru(zr, conv_w, conv_b, ga_w, ga_b, gx_w, gx_b, lam):
    b, s, c2 = zr.shape
    d = c2 // 2
    cb = RNN_BLOCK
    ncb = d // cb
    ts = min(256, s)
    assert s % ts == 0 and ts % SUBLANES == 0 and cb % LANES == 0
    n_half = cb // LANES
    vec = pl.BlockSpec((1, cb), lambda n, t: (0, n))
    wspec = pl.BlockSpec((None, cb, cb), lambda n, t: (n, 0, 0))
    blocks = 2 * _nbytes((b, ts, cb), F32) + _nbytes((b, ts, cb), BF16) + 2 * _nbytes((cb, cb), BF16)
    scratch = 4 * _nbytes((b, ts + SUBLANES, cb), F32)
    return pl.pallas_call(
        _rglru_kernel,
        grid=(ncb, s // ts),
        in_specs=[
            pl.BlockSpec((b, ts, cb), lambda n, t: (0, t, n)),
            pl.BlockSpec((b, ts, cb), lambda n, t: (0, t, n + ncb)),
            pl.BlockSpec((CONV_WIDTH, cb), lambda n, t: (0, n)),
            vec, wspec, vec, wspec, vec, vec,
        ],
        out_specs=pl.BlockSpec((b, ts, cb), lambda n, t: (0, t, n)),
        out_shape=jax.ShapeDtypeStruct((b, s, d), BF16),
        scratch_shapes=[
            pltpu.VMEM((b, ts + SUBLANES, cb), F32),
            pltpu.VMEM((n_half, b * ts, LANES), F32),
            pltpu.VMEM((n_half, b * ts, LANES), F32),
            pltpu.VMEM((n_half, b * ts, LANES), F32),
            pltpu.VMEM((n_half, b, LANES), F32),
        ],
        compiler_params=_params(("arbitrary", "arbitrary"), blocks + scratch // 2 + 8 * _nbytes((b, ts, cb), F32)),
        name="rglru",
    )(zr, zr, conv_w, conv_b.reshape(1, d), ga_w.astype(BF16), ga_b.reshape(1, d),
      gx_w.astype(BF16), gx_b.reshape(1, d), lam.reshape(1, d))


def _router_kernel(u_ref, wr_ref, meta_ref, cnt_ref, carry_scr):
    step = pl.program_id(0)

    @pl.when(step == 0)
    def _():
        carry_scr[...] = jnp.zeros(carry_scr.shape, F32)

    tr = u_ref.shape[0]
    logits = jnp.dot(u_ref[...], wr_ref[...], preferred_element_type=F32, precision=lax.Precision.HIGHEST)
    lane = lax.broadcasted_iota(jnp.int32, logits.shape, 1).astype(F32)
    logits = jnp.where(lane < N_EXPERTS, logits, -jnp.inf)
    m1 = jnp.max(logits, axis=-1, keepdims=True)
    i1 = jnp.min(jnp.where(logits == m1, lane, float(LANES)), axis=-1, keepdims=True)
    rest = jnp.where(lane == i1, -jnp.inf, logits)
    m2 = jnp.max(rest, axis=-1, keepdims=True)
    i2 = jnp.min(jnp.where(rest == m2, lane, float(LANES)), axis=-1, keepdims=True)
    e2 = jnp.exp(m2 - m1)
    den = 1.0 + e2
    w1 = 1.0 / den
    w2 = e2 / den
    hit = ((lane == i1) | (lane == i2))
    rows = lax.broadcasted_iota(jnp.int32, (tr, tr), 0)
    cols = lax.broadcasted_iota(jnp.int32, (tr, tr), 1)
    strict_lower = (cols < rows).astype(BF16)
    before = jnp.dot(strict_lower, hit.astype(BF16), preferred_element_type=F32) + carry_scr[...]
    r1 = jnp.sum(jnp.where(lane == i1, before, 0.0), axis=-1, keepdims=True)
    r2 = jnp.sum(jnp.where(lane == i2, before, 0.0), axis=-1, keepdims=True)
    total = carry_scr[...] + jnp.sum(hit.astype(F32), axis=0, keepdims=True)
    carry_scr[...] = total
    cnt_ref[...] = total
    meta = jnp.where(lane == 0, i1, 0.0)
    meta = jnp.where(lane == 1, i2, meta)
    meta = jnp.where(lane == 2, r1, meta)
    meta = jnp.where(lane == 3, r2, meta)
    meta = jnp.where(lane == 4, w1, meta)
    meta = jnp.where(lane == 5, w2, meta)
    meta_ref[...] = meta


def moe_router(u, w_router):
    t, d = u.shape
    tr = _row_tile(t)
    wr = jnp.zeros((d, LANES), F32).at[:, :N_EXPERTS].set(w_router)
    blocks = _nbytes((tr, d), F32) * 2 + _nbytes((tr, tr), F32)
    return pl.pallas_call(
        _router_kernel,
        grid=(t // tr,),
        in_specs=[pl.BlockSpec((tr, d), lambda i: (i, 0)), pl.BlockSpec((d, LANES), lambda i: (0, 0))],
        out_specs=[pl.BlockSpec((tr, LANES), lambda i: (i, 0)), pl.BlockSpec((1, LANES), lambda i: (0, 0))],
        out_shape=[jax.ShapeDtypeStruct((t, LANES), F32), jax.ShapeDtypeStruct((1, LANES), F32)],
        scratch_shapes=[pltpu.VMEM((1, LANES), F32)],
        compiler_params=_params(("arbitrary",), blocks),
        name="moe_router",
    )(u, wr)


def _row_copy(src_hbm, dst_vmem, sem, src_row, dst_row):
    return pltpu.make_async_copy(src_hbm.at[pl.ds(src_row, 1), :], dst_vmem.at[pl.ds(dst_row, 1), :], sem)


def _gather_kernel(idx_ref, src_hbm, o_ref, buf, sem):
    rows = buf.shape[0]

    def issue(i, carry):
        _row_copy(src_hbm, buf, sem, idx_ref[0, 0, i], i).start()
        return carry

    lax.fori_loop(0, rows, issue, 0)

    def drain(i, carry):
        _row_copy(src_hbm, buf, sem, 0, i).wait()
        return carry

    lax.fori_loop(0, rows, drain, 0)
    o_ref[...] = buf[...].astype(o_ref.dtype)


def gather_rows(src, idx, out_dtype):
    p = idx.shape[0]
    d = src.shape[1]
    rows = DMA_ROWS
    assert p % rows == 0
    blocks = _nbytes((rows, d), src.dtype) + _nbytes((rows, d), out_dtype)
    return pl.pallas_call(
        _gather_kernel,
        grid=(p // rows,),
        in_specs=[
            pl.BlockSpec((1, 1, rows), lambda i: (i, 0, 0), memory_space=pltpu.SMEM),
            pl.BlockSpec(memory_space=pl.ANY),
        ],
        out_specs=pl.BlockSpec((rows, d), lambda i: (i, 0)),
        out_shape=jax.ShapeDtypeStruct((p, d), out_dtype),
        scratch_shapes=[pltpu.VMEM((rows, d), src.dtype), pltpu.SemaphoreType.DMA(())],
        compiler_params=_params(("arbitrary",), blocks),
        name="moe_gather",
    )(idx.reshape(p // rows, 1, rows), src)


def _combine_kernel(p1_ref, p2_ref, meta_ref, src_hbm, o_ref, buf1, buf2, sem):
    rows = buf1.shape[0]

    def issue(i, carry):
        _row_copy(src_hbm, buf1, sem.at[0], p1_ref[0, 0, i], i).start()
        _row_copy(src_hbm, buf2, sem.at[1], p2_ref[0, 0, i], i).start()
        return carry

    lax.fori_loop(0, rows, issue, 0)

    def drain(i, carry):
        _row_copy(src_hbm, buf1, sem.at[0], 0, i).wait()
        _row_copy(src_hbm, buf2, sem.at[1], 0, i).wait()
        return carry

    lax.fori_loop(0, rows, drain, 0)
    meta = meta_ref[...]
    o_ref[...] = buf1[...] * meta[:, 4:5] + buf2[...] * meta[:, 5:6]


def combine_rows(yb, pos1, pos2, meta):
    t = pos1.shape[0]
    d = yb.shape[1]
    rows = DMA_ROWS
    assert t % rows == 0
    idx_spec = pl.BlockSpec((1, 1, rows), lambda i: (i, 0, 0), memory_space=pltpu.SMEM)
    blocks = 3 * _nbytes((rows, d), F32) + _nbytes((rows, LANES), F32)
    return pl.pallas_call(
        _combine_kernel,
        grid=(t // rows,),
        in_specs=[idx_spec, idx_spec, pl.BlockSpec((rows, LANES), lambda i: (i, 0)),
                  pl.BlockSpec(memory_space=pl.ANY)],
        out_specs=pl.BlockSpec((rows, d), lambda i: (i, 0)),
        out_shape=jax.ShapeDtypeStruct((t, d), F32),
        scratch_shapes=[pltpu.VMEM((rows, d), F32), pltpu.VMEM((rows, d), F32), pltpu.SemaphoreType.DMA((2,))],
        compiler_params=_params(("arbitrary",), blocks),
        name="moe_combine",
    )(pos1.reshape(t // rows, 1, rows), pos2.reshape(t // rows, 1, rows), meta, yb)


def _expert_swiglu_kernel(te_ref, tv_ref, x_ref, wg_ref, wu_ref, o_ref):
    i = pl.program_id(1)

    @pl.when(tv_ref[i] > 0)
    def _():
        _swiglu_kernel(x_ref, wg_ref, wu_ref, o_ref)

    @pl.when(tv_ref[i] == 0)
    def _():
        o_ref[...] = jnp.zeros(o_ref.shape, o_ref.dtype)


def expert_swiglu_in(xs, w_in, tile_expert, tile_valid):
    p, k = xs.shape
    f = w_in.shape[2] // 2
    tn = _ff_tile(f)
    tm = MOE_TILE_ROWS
    nf = f // tn
    blocks = _nbytes((tm, k), BF16) + 2 * _nbytes((k, tn), BF16) + _nbytes((tm, tn), BF16) + _nbytes((tm, tn), F32)
    grid_spec = pltpu.PrefetchScalarGridSpec(
        num_scalar_prefetch=2,
        grid=(nf, p // tm),
        in_specs=[
            pl.BlockSpec((tm, k), lambda j, i, te, tv: (i, 0)),
            pl.BlockSpec((None, k, tn), lambda j, i, te, tv: (te[i], 0, j)),
            pl.BlockSpec((None, k, tn), lambda j, i, te, tv: (te[i], 0, j + nf)),
        ],
        out_specs=pl.BlockSpec((tm, tn), lambda j, i, te, tv: (i, j)),
    )
    return pl.pallas_call(
        _expert_swiglu_kernel,
        grid_spec=grid_spec,
        out_shape=jax.ShapeDtypeStruct((p, f), BF16),
        compiler_params=_params(("arbitrary", "arbitrary"), blocks),
        name="expert_swiglu_in",
    )(tile_expert, tile_valid, xs, w_in, w_in)


def _expert_out_kernel(te_ref, tv_ref, h_ref, w_ref, o_ref):
    i = pl.program_id(1)

    @pl.when(tv_ref[i] > 0)
    def _():
        _mm_kernel(h_ref, w_ref, o_ref)

    @pl.when(tv_ref[i] == 0)
    def _():
        o_ref[...] = jnp.zeros(o_ref.shape, o_ref.dtype)


def expert_out(h, w_out, tile_expert, tile_valid):
    p, f = h.shape
    d = w_out.shape[2]
    tm = MOE_TILE_ROWS
    tn = min(1024, d)
    blocks = _nbytes((tm, f), BF16) + _nbytes((f, tn), BF16) + _nbytes((tm, tn), F32) * 2
    grid_spec = pltpu.PrefetchScalarGridSpec(
        num_scalar_prefetch=2,
        grid=(d // tn, p // tm),
        in_specs=[
            pl.BlockSpec((tm, f), lambda j, i, te, tv: (i, 0)),
            pl.BlockSpec((None, f, tn), lambda j, i, te, tv: (te[i], 0, j)),
        ],
        out_specs=pl.BlockSpec((tm, tn), lambda j, i, te, tv: (i, j)),
    )
    return pl.pallas_call(
        _expert_out_kernel,
        grid_spec=grid_spec,
        out_shape=jax.ShapeDtypeStruct((p, d), F32),
        compiler_params=_params(("arbitrary", "arbitrary"), blocks),
        name="expert_out",
    )(tile_expert, tile_valid, h, w_out)


def moe_swiglu(u, w_router, w_in, w_out):
    t, d = u.shape
    tm = MOE_TILE_ROWS
    n_tiles = (t * TOP_K) // tm + N_EXPERTS
    p = n_tiles * tm
    meta, counts = moe_router(u, w_router)
    cnt = counts[0, :N_EXPERTS].astype(jnp.int32)
    tiles_e = (cnt + tm - 1) // tm
    tile_end = jnp.cumsum(tiles_e)
    row_start = (tile_end - tiles_e) * tm
    tile_ids = jnp.arange(n_tiles, dtype=jnp.int32)
    tile_expert = jnp.minimum(jnp.searchsorted(tile_end, tile_ids, side='right'), N_EXPERTS - 1).astype(jnp.int32)
    tile_valid = (tile_ids < tile_end[-1]).astype(jnp.int32)
    e1 = meta[:, 0].astype(jnp.int32)
    e2 = meta[:, 1].astype(jnp.int32)
    pos1 = row_start[e1] + meta[:, 2].astype(jnp.int32)
    pos2 = row_start[e2] + meta[:, 3].astype(jnp.int32)
    tok = jnp.arange(t, dtype=jnp.int32)
    row_tok = jnp.zeros((p,), jnp.int32).at[pos1].set(tok).at[pos2].set(tok)
    xs = gather_rows(u, row_tok, BF16)
    h = expert_swiglu_in(xs, w_in, tile_expert, tile_valid)
    yb = expert_out(h, w_out, tile_expert, tile_valid)
    return combine_rows(yb, pos1, pos2, meta)


def kernel(x, c, ada_w, ada_b, ln_g, ln_b, attn_w_qkv, attn_w_o, rg_w_in, rg_conv_w, rg_conv_b,
           rg_gate_a_w, rg_gate_a_b, rg_gate_x_w, rg_gate_x_b, rg_lambda, rg_w_out,
           ffn_w_in, ffn_w_out, moe_w_router, moe_w_in, moe_w_out):
    b, s, d = x.shape
    depth = ada_w.shape[0]
    t = b * s
    alpha = (2 * depth) ** 0.25
    x = x.astype(F32)

    mod = ada_modulation(c, ada_w, ada_b)
    mod4 = [mod[i].reshape(b, N_MOD, 1, d) for i in range(depth)]
    SH1, SC1, G1, SH2, SC2, G2 = range(N_MOD)

    u = modulate(x, mod4[0], SC1, SH1, BF16)
    for i in range(depth):
        j = i // 2
        is_attn = i % 2 == 0
        if is_attn:
            qkv = matmul(u.reshape(t, d), attn_w_qkv[j].astype(BF16), BF16, 1024, 1024).reshape(b, s, -1)
            outs, lses = [], []
            for g, (window, dilation) in enumerate(ATTN_GROUPS):
                o, lse = dilated_attention_group(qkv, g, window, dilation)
                outs.append(o)
                lses.append(lse)
            merged = merge_groups(outs, lses)
            y = matmul(merged, attn_w_o[j].astype(BF16), F32, 1024, 1024)
        else:
            zr = matmul(u.reshape(t, d), rg_w_in[j].astype(BF16), F32, 1024, 1024).reshape(b, s, -1)
            yr = rglru(zr, rg_conv_w[j], rg_conv_b[j], rg_gate_a_w[j], rg_gate_a_b[j],
                       rg_gate_x_w[j], rg_gate_x_b[j], rg_lambda[j])
            y = matmul(yr.reshape(t, d), rg_w_out[j].astype(BF16), F32, 1024, 1024)
        x, u = residual_layer_norm(x, y, mod4[i], G1, ln_g[i, 0], ln_b[i, 0], alpha,
                                   next_mod4=(mod4[i], SC2, SH2), u_dtype=BF16 if is_attn else F32)
        if is_attn:
            h = swiglu_in(u.reshape(t, d), ffn_w_in[j].astype(BF16), 1024)
            y = matmul(h, ffn_w_out[j].astype(BF16), F32, 512, 1024)
        else:
            y = moe_swiglu(u.reshape(t, d), moe_w_router[j], moe_w_in[j].astype(BF16), moe_w_out[j].astype(BF16))
        nxt = (mod4[i + 1], SC1, SH1) if i + 1 < depth else None
        x, u = residual_layer_norm(x, y, mod4[i], G2, ln_g[i, 1], ln_b[i, 1], alpha, next_mod4=nxt, u_dtype=BF16)
    return x
```

```python
import functools

import jax
import jax.numpy as jnp
from jax import lax
from jax.experimental import pallas as pl
from jax.experimental.pallas import tpu as pltpu

F32 = jnp.float32
BF16 = jnp.bfloat16

N_HEADS = 16
ATTN_GROUPS = ((128, 1), (512, 4), (2048, 16))
Q_BLOCK = 128
RNN_BLOCK = 256
CONV_WIDTH = 4
LRU_C = 8.0
N_EXPERTS = 8
TOP_K = 2
LN_EPS = 1e-5
NEG_INF = -1e30
N_MOD = 6

LANES = 128
SUBLANES = 8
VMEM_BYTES = 64 * 1024 * 1024

MOE_TILE_ROWS = 512
DMA_ROWS = 256


def _vmem_limit(block_bytes):
    want = 2 * block_bytes + 12 * 1024 * 1024
    return int(min(want, VMEM_BYTES - 6 * 1024 * 1024))


def _params(semantics, block_bytes):
    return pltpu.CompilerParams(dimension_semantics=semantics,
                                vmem_limit_bytes=_vmem_limit(block_bytes))


def _nbytes(shape, dtype):
    n = 1
    for s in shape:
        n *= s
    return n * jnp.dtype(dtype).itemsize


def _ada_kernel(c_ref, w_ref, b_ref, o_ref):
    c = c_ref[...]
    c_act = (c * jax.nn.sigmoid(c)).astype(BF16)
    o_ref[...] = jnp.dot(c_act, w_ref[...].astype(BF16), preferred_element_type=F32) + b_ref[...]


def ada_modulation(c, ada_w, ada_b):
    depth, d, n = ada_w.shape
    b = c.shape[0]
    tn = 1536 if n % 1536 == 0 else n
    blocks = _nbytes((d, tn), F32) + _nbytes((b, tn), F32) * 2
    return pl.pallas_call(
        _ada_kernel,
        grid=(depth, n // tn),
        in_specs=[
            pl.BlockSpec((b, d), lambda i, j: (0, 0)),
            pl.BlockSpec((None, d, tn), lambda i, j: (i, 0, j)),
            pl.BlockSpec((None, 1, tn), lambda i, j: (i, 0, j)),
        ],
        out_specs=pl.BlockSpec((None, b, tn), lambda i, j: (i, 0, j)),
        out_shape=jax.ShapeDtypeStruct((depth, b, n), F32),
        compiler_params=_params(("arbitrary", "arbitrary"), blocks),
        name="ada_modulation",
    )(c, ada_w, ada_b.reshape(depth, 1, n))


def _modulate_kernel(x_ref, sc_ref, sh_ref, u_ref):
    u_ref[...] = (x_ref[...] * (1.0 + sc_ref[...]) + sh_ref[...]).astype(u_ref.dtype)


def _ln_kernel(x_ref, y_ref, gate_ref, lng_ref, lnb_ref, *rest, alpha, with_u):
    if with_u:
        sc_ref, sh_ref, xo_ref, u_ref = rest
    else:
        (xo_ref,) = rest
    z = alpha * x_ref[...] + (1.0 + gate_ref[...]) * y_ref[...].astype(F32)
    mu = jnp.mean(z, axis=-1, keepdims=True)
    zc = z - mu
    var = jnp.mean(zc * zc, axis=-1, keepdims=True)
    xn = zc * lax.rsqrt(var + LN_EPS) * lng_ref[...] + lnb_ref[...]
    xo_ref[...] = xn
    if with_u:
        u_ref[...] = (xn * (1.0 + sc_ref[...]) + sh_ref[...]).astype(u_ref.dtype)


def _row_tile(s):
    for t in (512, 256, 128, 64, 32, 16, 8):
        if s % t == 0:
            return t
    return s


def modulate(x, mod4, sc_chunk, sh_chunk, u_dtype):
    b, s, d = x.shape
    tm = _row_tile(s)
    vec = lambda ch: pl.BlockSpec((None, None, 1, d), lambda bi, ti, ch=ch: (bi, ch, 0, 0))
    row = pl.BlockSpec((None, tm, d), lambda bi, ti: (bi, ti, 0))
    blocks = _nbytes((tm, d), F32) * 2
    return pl.pallas_call(
        _modulate_kernel,
        grid=(b, s // tm),
        in_specs=[row, vec(sc_chunk), vec(sh_chunk)],
        out_specs=row,
        out_shape=jax.ShapeDtypeStruct((b, s, d), u_dtype),
        compiler_params=_params(("arbitrary", "arbitrary"), blocks),
        name="modulate",
    )(x, mod4, mod4)


def residual_layer_norm(x, y, mod4, gate_chunk, ln_g, ln_b, alpha, next_mod4=None, u_dtype=BF16):
    b, s, d = x.shape
    tm = _row_tile(s)
    with_u = next_mod4 is not None
    vec = lambda ch: pl.BlockSpec((None, None, 1, d), lambda bi, ti, ch=ch: (bi, ch, 0, 0))
    row = pl.BlockSpec((None, tm, d), lambda bi, ti: (bi, ti, 0))
    par = pl.BlockSpec((1, d), lambda bi, ti: (0, 0))
    in_specs = [row, row, vec(gate_chunk), par, par]
    args = [x, y.reshape(b, s, d), mod4, ln_g.reshape(1, d), ln_b.reshape(1, d)]
    out_specs = [row]
    out_shape = [jax.ShapeDtypeStruct((b, s, d), F32)]
    n_rows = 3
    if with_u:
        nm4, sc_chunk, sh_chunk = next_mod4
        in_specs += [vec(sc_chunk), vec(sh_chunk)]
        args += [nm4, nm4]
        out_specs.append(row)
        out_shape.append(jax.ShapeDtypeStruct((b, s, d), u_dtype))
        n_rows = 4
    out = pl.pallas_call(
        functools.partial(_ln_kernel, alpha=alpha, with_u=with_u),
        grid=(b, s // tm),
        in_specs=in_specs,
        out_specs=out_specs,
        out_shape=out_shape,
        compiler_params=_params(("arbitrary", "arbitrary"), _nbytes((tm, d), F32) * n_rows),
        name="residual_layer_norm",
    )(*args)
    return (out[0], out[1]) if with_u else (out[0], None)


def _mm_kernel(x_ref, w_ref, o_ref):
    o_ref[...] = jnp.dot(x_ref[...], w_ref[...], preferred_element_type=F32).astype(o_ref.dtype)


def matmul(x, w, out_dtype, tm, tn):
    m, k = x.shape
    n = w.shape[1]
    tm = min(tm, m)
    tn = min(tn, n)
    blocks = _nbytes((tm, k), BF16) + _nbytes((k, tn), BF16) + _nbytes((tm, tn), out_dtype) + _nbytes((tm, tn), F32) // 2
    return pl.pallas_call(
        _mm_kernel,
        grid=(n // tn, m // tm),
        in_specs=[
            pl.BlockSpec((tm, k), lambda j, i: (i, 0)),
            pl.BlockSpec((k, tn), lambda j, i: (0, j)),
        ],
        out_specs=pl.BlockSpec((tm, tn), lambda j, i: (i, j)),
        out_shape=jax.ShapeDtypeStruct((m, n), out_dtype),
        compiler_params=_params(("arbitrary", "arbitrary"), blocks),
        name="matmul",
    )(x, w)


def _swiglu_kernel(x_ref, wg_ref, wu_ref, o_ref):
    x = x_ref[...]
    g = jnp.dot(x, wg_ref[...], preferred_element_type=F32)
    up = jnp.dot(x, wu_ref[...], preferred_element_type=F32)
    o_ref[...] = (g * jax.nn.sigmoid(g) * up).astype(o_ref.dtype)


def _ff_tile(f):
    for t in (512, 256, 128):
        if f % t == 0:
            return t
    return f


def swiglu_in(x, w_in, tm):
    m, k = x.shape
    f = w_in.shape[1] // 2
    tn = _ff_tile(f)
    tm = min(tm, m)
    nf = f // tn
    blocks = _nbytes((tm, k), BF16) + 2 * _nbytes((k, tn), BF16) + _nbytes((tm, tn), BF16) + _nbytes((tm, tn), F32)
    return pl.pallas_call(
        _swiglu_kernel,
        grid=(nf, m // tm),
        in_specs=[
            pl.BlockSpec((tm, k), lambda j, i: (i, 0)),
            pl.BlockSpec((k, tn), lambda j, i: (0, j)),
            pl.BlockSpec((k, tn), lambda j, i: (0, j + nf)),
        ],
        out_specs=pl.BlockSpec((tm, tn), lambda j, i: (i, j)),
        out_shape=jax.ShapeDtypeStruct((m, f), BF16),
        compiler_params=_params(("arbitrary", "arbitrary"), blocks),
        name="swiglu_in",
    )(x, w_in, w_in)


def _qkv_kernel(x_ref, w_ref, o_ref, acc_scr, *, dilation):
    res = jnp.dot(x_ref[...], w_ref[...], preferred_element_type=F32)
    tm, tn = res.shape
    if dilation == 1:
        o_ref[0] = res.astype(o_ref.dtype)
        return
    for c in range(tn // LANES):
        acc_scr[c] = res[:, c * LANES:(c + 1) * LANES]
    rows = tm // dilation
    for r in range(dilation):
        for c in range(tn // LANES):
            piece = acc_scr[c, pl.ds(r, rows, stride=dilation), :]
            o_ref[r, :, c * LANES:(c + 1) * LANES] = piece.astype(o_ref.dtype)


def qkv_projection(u, w_qkv, g, dilation):
    t, d = u.shape
    tm = 512
    assert t % tm == 0 and tm % (dilation * 2 * SUBLANES) == 0 and d % LANES == 0
    blocks = _nbytes((tm, d), BF16) * 2 + _nbytes((d, d), BF16) + _nbytes((tm, d), F32) * 2
    return pl.pallas_call(
        functools.partial(_qkv_kernel, dilation=dilation),
        grid=(3, t // tm),
        in_specs=[
            pl.BlockSpec((tm, d), lambda j, i: (i, 0)),
            pl.BlockSpec((d, d), lambda j, i: (0, g * 3 + j)),
        ],
        out_specs=pl.BlockSpec((dilation, None, tm // dilation, d), lambda j, i: (0, j, i, 0)),
        out_shape=jax.ShapeDtypeStruct((dilation, 3, t // dilation, d), BF16),
        scratch_shapes=[pltpu.VMEM((d // LANES, tm, LANES), F32)],
        compiler_params=_params(("arbitrary", "arbitrary"), blocks),
        name=f"qkv_projection_g{g}",
    )(u, w_qkv)


def _attn_kernel(q_ref, kp_ref, kc_ref, vp_ref, vc_ref, o_ref, lse_ref, s_scr, p_scr, m_scr, linv_scr,
                 *, dilation, steps, head_dim):
    n = pl.program_id(2)
    qb = q_ref.shape[0]
    qi = lax.broadcasted_iota(jnp.int32, (qb, 2 * qb), 0)
    ki = lax.broadcasted_iota(jnp.int32, (qb, 2 * qb), 1)
    rel = qi + qb - ki
    no_prev = jnp.where(n > 0, 0, 2 * steps + 2 * qb)
    rel_m = jnp.where(ki < qb, rel + no_prev, rel)
    valid = (rel_m >= 0) & (rel_m <= steps)
    base = -(rel * dilation).astype(F32)
    scale = head_dim ** -0.5
    contract_last = (((1,), (1,)), ((), ()))
    heads = [slice(h * head_dim, (h + 1) * head_dim) for h in range(N_HEADS)]

    for h, hs in enumerate(heads):
        k = jnp.concatenate([kp_ref[:, hs], kc_ref[:, hs]], axis=0)
        s_scr[h] = lax.dot_general(q_ref[:, hs], k, contract_last, preferred_element_type=F32)
    for h in range(N_HEADS):
        slope = 2.0 ** (-8.0 * (h + 1) / N_HEADS)
        s = jnp.where(valid, s_scr[h] * scale + slope * base, NEG_INF)
        s_scr[h] = s
        m_scr[h] = jnp.broadcast_to(jnp.max(s, axis=-1, keepdims=True), m_scr.shape[1:])
    lane = lax.broadcasted_iota(jnp.int32, lse_ref.shape, 1)
    lse_tile = jnp.zeros(lse_ref.shape, F32)
    for h in range(N_HEADS):
        m = m_scr[h]
        p = jnp.exp(s_scr[h] - jnp.concatenate([m, m], axis=1))
        l = jnp.sum(p, axis=-1, keepdims=True)
        p_scr[h] = p.astype(BF16)
        linv_scr[h] = jnp.broadcast_to(1.0 / l, linv_scr.shape[1:])
        lse_tile = jnp.where(lane == h, m + jnp.log(l), lse_tile)
    lse_ref[...] = lse_tile
    for h, hs in enumerate(heads):
        v = jnp.concatenate([vp_ref[:, hs], vc_ref[:, hs]], axis=0)
        o = jnp.dot(p_scr[h], v, preferred_element_type=F32)
        o_ref[:, hs] = (o * linv_scr[h]).astype(o_ref.dtype)


def dilated_attention_group(qkv_g, g, window, dilation, batch):
    _, _, rows, d = qkv_g.shape
    head_dim = d // N_HEADS
    steps = window // dilation
    l = rows // batch
    assert steps <= Q_BLOCK and l % Q_BLOCK == 0 and N_HEADS <= LANES and head_dim == LANES
    nb = l // Q_BLOCK
    cur = lambda j: (lambda bi, r, n: (r, j, bi * nb + n, 0))
    prev = lambda j: (lambda bi, r, n: (r, j, bi * nb + jnp.maximum(n - 1, 0), 0))
    spec = lambda im: pl.BlockSpec((None, None, Q_BLOCK, d), im)
    blocks = 6 * _nbytes((Q_BLOCK, d), BF16) + _nbytes((Q_BLOCK, LANES), F32)
    scratch = (_nbytes((N_HEADS, Q_BLOCK, 2 * Q_BLOCK), F32) + _nbytes((N_HEADS, Q_BLOCK, 2 * Q_BLOCK), BF16)
               + 2 * _nbytes((N_HEADS, Q_BLOCK, LANES), F32))
    return pl.pallas_call(
        functools.partial(_attn_kernel, dilation=dilation, steps=steps, head_dim=head_dim),
        grid=(batch, dilation, nb),
        in_specs=[spec(cur(0)), spec(prev(1)), spec(cur(1)), spec(prev(2)), spec(cur(2))],
        out_specs=[
            pl.BlockSpec((None, Q_BLOCK, d), lambda bi, r, n: (r, bi * nb + n, 0)),
            pl.BlockSpec((None, Q_BLOCK, LANES), lambda bi, r, n: (r, bi * nb + n, 0)),
        ],
        out_shape=[
            jax.ShapeDtypeStruct((dilation, rows, d), BF16),
            jax.ShapeDtypeStruct((dilation, rows, LANES), F32),
        ],
        scratch_shapes=[
            pltpu.VMEM((N_HEADS, Q_BLOCK, 2 * Q_BLOCK), F32),
            pltpu.VMEM((N_HEADS, Q_BLOCK, 2 * Q_BLOCK), BF16),
            pltpu.VMEM((N_HEADS, Q_BLOCK, LANES), F32),
            pltpu.VMEM((N_HEADS, Q_BLOCK, LANES), F32),
        ],
        compiler_params=_params(("arbitrary", "arbitrary", "arbitrary"), blocks + scratch),
        name=f"dilated_attention_g{g}",
    )(qkv_g, qkv_g, qkv_g, qkv_g, qkv_g)


def _merge_kernel(o0_ref, o1_ref, o2_ref, l0_ref, l1_ref, l2_ref, out_ref, nat1_scr, nat2_scr, lse1_scr, lse2_scr,
                  *, head_dim):
    tm = out_ref.shape[0]
    for o_ref, l_ref, nat_scr, lse_scr in ((o1_ref, l1_ref, nat1_scr, lse1_scr), (o2_ref, l2_ref, nat2_scr, lse2_scr)):
        dilation = o_ref.shape[0]
        rows = tm // dilation
        for r in range(dilation):
            lse_scr[pl.ds(r, rows, stride=dilation), :] = l_ref[r]
            for h in range(N_HEADS):
                nat_scr[h, pl.ds(r, rows, stride=dilation), :] = o_ref[r, :, h * head_dim:(h + 1) * head_dim].astype(F32)
    l0, l1, l2 = l0_ref[0], lse1_scr[...], lse2_scr[...]
    m = jnp.maximum(jnp.maximum(l0, l1), l2)
    e0, e1, e2 = jnp.exp(l0 - m), jnp.exp(l1 - m), jnp.exp(l2 - m)
    den = e0 + e1 + e2
    w0, w1, w2 = e0 / den, e1 / den, e2 / den
    for h in range(N_HEADS):
        hs = slice(h * head_dim, (h + 1) * head_dim)
        acc = o0_ref[0, :, hs].astype(F32) * w0[:, h:h + 1]
        acc = acc + nat1_scr[h] * w1[:, h:h + 1]
        acc = acc + nat2_scr[h] * w2[:, h:h + 1]
        out_ref[:, hs] = acc.astype(out_ref.dtype)


def merge_groups(outs, lses):
    dils = [o.shape[0] for o in outs]
    assert dils[0] == 1 and len(outs) == 3
    t, d = outs[0].shape[1:]
    head_dim = d // N_HEADS
    tm = 512
    assert t % tm == 0 and all(tm % (dl * 2 * SUBLANES) == 0 for dl in dils) and head_dim == LANES
    ospec = lambda dl: pl.BlockSpec((dl, tm // dl, d), lambda i: (0, i, 0))
    lspec = lambda dl: pl.BlockSpec((dl, tm // dl, LANES), lambda i: (0, i, 0))
    blocks = 4 * _nbytes((tm, d), BF16) + 3 * _nbytes((tm, LANES), F32) + 3 * _nbytes((tm, d), F32)
    return pl.pallas_call(
        functools.partial(_merge_kernel, head_dim=head_dim),
        grid=(t // tm,),
        in_specs=[ospec(dl) for dl in dils] + [lspec(dl) for dl in dils],
        out_specs=pl.BlockSpec((tm, d), lambda i: (i, 0)),
        out_shape=jax.ShapeDtypeStruct((t, d), BF16),
        scratch_shapes=[
            pltpu.VMEM((N_HEADS, tm, LANES), F32),
            pltpu.VMEM((N_HEADS, tm, LANES), F32),
            pltpu.VMEM((tm, LANES), F32),
            pltpu.VMEM((tm, LANES), F32),
        ],
        compiler_params=_params(("arbitrary",), blocks),
        name="merge_groups",
    )(*outs, *lses)


def _softplus(x):
    return jnp.maximum(x, 0.0) + jnp.log1p(jnp.exp(-jnp.abs(x)))


def _rglru_kernel(gate_ref, rec_ref, cw_ref, cb_ref, gaw_ref, gab_ref, gxw_ref, gxb_ref, lam_ref,
                  y_ref, ext_scr, a_scr, b_scr, h_scr, carry_scr):
    t = pl.program_id(1)
    nb, ts, cb = rec_ref.shape
    halo = SUBLANES
    n_half = cb // LANES

    @pl.when(t == 0)
    def _():
        ext_scr[:, 0:halo, :] = jnp.zeros((nb, halo, cb), F32)
        carry_scr[...] = jnp.zeros(carry_scr.shape, F32)

    rec = rec_ref[...]
    ext_scr[:, halo:halo + ts, :] = rec
    xc = cb_ref[...].reshape(1, 1, cb)
    for k in range(CONV_WIDTH):
        off = halo - (CONV_WIDTH - 1) + k
        xc = xc + cw_ref[k:k + 1, :].reshape(1, 1, cb) * ext_scr[:, off:off + ts, :]
    ext_scr[:, 0:halo, :] = rec[:, ts - halo:ts, :]

    xc2 = xc.reshape(nb * ts, cb)
    xb = xc2.astype(BF16)
    r = jax.nn.sigmoid(jnp.dot(xb, gaw_ref[...], preferred_element_type=F32) + gab_ref[...])
    i = jax.nn.sigmoid(jnp.dot(xb, gxw_ref[...], preferred_element_type=F32) + gxb_ref[...])
    log_a = -LRU_C * r * _softplus(-lam_ref[...])
    a = jnp.exp(log_a)
    bb = jnp.sqrt(-jnp.tanh(log_a) * (a * a + 1.0)) * (i * xc2)
    for c in range(n_half):
        a_scr[c] = a[:, c * LANES:(c + 1) * LANES]
        b_scr[c] = bb[:, c * LANES:(c + 1) * LANES]

    def step(j, hs):
        out = []
        for c in range(n_half):
            a_t = a_scr[c, pl.ds(j, nb, stride=ts), :]
            b_t = b_scr[c, pl.ds(j, nb, stride=ts), :]
            h = a_t * hs[c] + b_t
            h_scr[c, pl.ds(j, nb, stride=ts), :] = h
            out.append(h)
        return tuple(out)

    h0 = tuple(carry_scr[c] for c in range(n_half))
    hn = lax.fori_loop(0, ts, step, h0, unroll=8)
    for c in range(n_half):
        carry_scr[c] = hn[c]

    gate = jax.nn.gelu(gate_ref[...]).reshape(nb * ts, cb)
    for c in range(n_half):
        cs = slice(c * LANES, (c + 1) * LANES)
        y_ref[:, :, cs] = (gate[:, cs] * h_scr[c]).reshape(nb, ts, LANES).astype(y_ref.dtype)


def rglru(zr, conv_w, conv_b, ga_w, ga_b, gx_w, gx_b, lam):
    b, s, c2 = zr.shape
    d = c2 // 2
    cb = RNN_BLOCK
    ncb = d // cb
    ts = min(256, s)
    assert s % ts == 0 and ts % SUBLANES == 0 and cb % LANES == 0
    n_half = cb // LANES
    vec = pl.BlockSpec((1, cb), lambda n, t: (0, n))
    wspec = pl.BlockSpec((None, cb, cb), lambda n, t: (n, 0, 0))
    blocks = 2 * _nbytes((b, ts, cb), F32) + _nbytes((b, ts, cb), BF16) + 2 * _nbytes((cb, cb), BF16)
    scratch = 4 * _nbytes((b, ts + SUBLANES, cb), F32)
    return pl.pallas_call(
        _rglru_kernel,
        grid=(ncb, s // ts),
        in_specs=[
            pl.BlockSpec((b, ts, cb), lambda n, t: (0, t, n)),
            pl.BlockSpec((b, ts, cb), lambda n, t: (0, t, n + ncb)),
            pl.BlockSpec((CONV_WIDTH, cb), lambda n, t: (0, n)),
            vec, wspec, vec, wspec, vec, vec,
        ],
        out_specs=pl.BlockSpec((b, ts, cb), lambda n, t: (0, t, n)),
        out_shape=jax.ShapeDtypeStruct((b, s, d), BF16),
        scratch_shapes=[
            pltpu.VMEM((b, ts + SUBLANES, cb), F32),
            pltpu.VMEM((n_half, b * ts, LANES), F32),
            pltpu.VMEM((n_half, b * ts, LANES), F32),
            pltpu.VMEM((n_half, b * ts, LANES), F32),
            pltpu.VMEM((n_half, b, LANES), F32),
        ],
        compiler_params=_params(("arbitrary", "arbitrary"), blocks + scratch // 2 + 8 * _nbytes((b, ts, cb), F32)),
        name="rglru",
    )(zr, zr, conv_w, conv_b.reshape(1, d), ga_w.astype(BF16), ga_b.reshape(1, d),
      gx_w.astype(BF16), gx_b.reshape(1, d), lam.reshape(1, d))


def _router_kernel(u_ref, wr_ref, meta_ref, cnt_ref, carry_scr):
    step = pl.program_id(0)

    @pl.when(step == 0)
    def _():
        carry_scr[...] = jnp.zeros(carry_scr.shape, F32)

    tr = u_ref.shape[0]
    logits = jnp.dot(u_ref[...], wr_ref[...], preferred_element_type=F32, precision=lax.Precision.HIGHEST)
    lane = lax.broadcasted_iota(jnp.int32, logits.shape, 1).astype(F32)
    logits = jnp.where(lane < N_EXPERTS, logits, -jnp.inf)
    m1 = jnp.max(logits, axis=-1, keepdims=True)
    i1 = jnp.min(jnp.where(logits == m1, lane, float(LANES)), axis=-1, keepdims=True)
    rest = jnp.where(lane == i1, -jnp.inf, logits)
    m2 = jnp.max(rest, axis=-1, keepdims=True)
    i2 = jnp.min(jnp.where(rest == m2, lane, float(LANES)), axis=-1, keepdims=True)
    e2 = jnp.exp(m2 - m1)
    den = 1.0 + e2
    w1 = 1.0 / den
    w2 = e2 / den
    hit = ((lane == i1) | (lane == i2))
    rows = lax.broadcasted_iota(jnp.int32, (tr, tr), 0)
    cols = lax.broadcasted_iota(jnp.int32, (tr, tr), 1)
    strict_lower = (cols < rows).astype(BF16)
    before = jnp.dot(strict_lower, hit.astype(BF16), preferred_element_type=F32) + carry_scr[...]
    r1 = jnp.sum(jnp.where(lane == i1, before, 0.0), axis=-1, keepdims=True)
    r2 = jnp.sum(jnp.where(lane == i2, before, 0.0), axis=-1, keepdims=True)
    total = carry_scr[...] + jnp.sum(hit.astype(F32), axis=0, keepdims=True)
    carry_scr[...] = total
    cnt_ref[...] = total
    meta = jnp.where(lane == 0, i1, 0.0)
    meta = jnp.where(lane == 1, i2, meta)
    meta = jnp.where(lane == 2, r1, meta)
    meta = jnp.where(lane == 3, r2, meta)
    meta = jnp.where(lane == 4, w1, meta)
    meta = jnp.where(lane == 5, w2, meta)
    meta_ref[...] = meta


def moe_router(u, w_router):
    t, d = u.shape
    tr = _row_tile(t)
    wr = jnp.zeros((d, LANES), F32).at[:, :N_EXPERTS].set(w_router)
    blocks = _nbytes((tr, d), F32) * 2 + _nbytes((tr, tr), F32)
    return pl.pallas_call(
        _router_kernel,
        grid=(t // tr,),
        in_specs=[pl.BlockSpec((tr, d), lambda i: (i, 0)), pl.BlockSpec((d, LANES), lambda i: (0, 0))],
        out_specs=[pl.BlockSpec((tr, LANES), lambda i: (i, 0)), pl.BlockSpec((1, LANES), lambda i: (0, 0))],
        out_shape=[jax.ShapeDtypeStruct((t, LANES), F32), jax.ShapeDtypeStruct((1, LANES), F32)],
        scratch_shapes=[pltpu.VMEM((1, LANES), F32)],
        compiler_params=_params(("arbitrary",), blocks),
        name="moe_router",
    )(u, wr)


def _row_copy(src_hbm, dst_vmem, sem, src_row, dst_row):
    return pltpu.make_async_copy(src_hbm.at[pl.ds(src_row, 1), :], dst_vmem.at[pl.ds(dst_row, 1), :], sem)


def _issue_rows(src_hbm, dst_vmem, sem, idx_ref, rows):
    def body(r, carry):
        _row_copy(src_hbm, dst_vmem, sem, idx_ref[0, 0, r], r).start()
        return carry

    lax.fori_loop(0, rows, body, 0, unroll=8)


def _wait_rows(src_hbm, dst_vmem, sem, rows):
    pltpu.make_async_copy(src_hbm.at[pl.ds(0, rows), :], dst_vmem, sem).wait()


def _gather_kernel(idx_ref, idx_next_ref, src_hbm, o_ref, buf, sem):
    i = pl.program_id(0)
    rows = o_ref.shape[0]
    slot = i % 2

    @pl.when(i == 0)
    def _():
        _issue_rows(src_hbm, buf.at[0], sem.at[0], idx_ref, rows)

    @pl.when(i + 1 < pl.num_programs(0))
    def _():
        _issue_rows(src_hbm, buf.at[1 - slot], sem.at[1 - slot], idx_next_ref, rows)

    _wait_rows(src_hbm, buf.at[slot], sem.at[slot], rows)
    o_ref[...] = buf[slot].astype(o_ref.dtype)


def gather_rows(src, idx, out_dtype):
    p = idx.shape[0]
    d = src.shape[1]
    rows = DMA_ROWS
    assert p % rows == 0
    steps = p // rows
    blocks = _nbytes((rows, d), src.dtype) + _nbytes((rows, d), out_dtype)
    idx3 = idx.reshape(steps, 1, rows)
    return pl.pallas_call(
        _gather_kernel,
        grid=(steps,),
        in_specs=[
            pl.BlockSpec((1, 1, rows), lambda i: (i, 0, 0), memory_space=pltpu.SMEM),
            pl.BlockSpec((1, 1, rows), lambda i: (jnp.minimum(i + 1, steps - 1), 0, 0), memory_space=pltpu.SMEM),
            pl.BlockSpec(memory_space=pl.ANY),
        ],
        out_specs=pl.BlockSpec((rows, d), lambda i: (i, 0)),
        out_shape=jax.ShapeDtypeStruct((p, d), out_dtype),
        scratch_shapes=[pltpu.VMEM((2, rows, d), src.dtype), pltpu.SemaphoreType.DMA((2,))],
        compiler_params=_params(("arbitrary",), blocks),
        name="moe_gather",
    )(idx3, idx3, src)


def _combine_kernel(p1_ref, p2_ref, p1_next_ref, p2_next_ref, meta_ref, src_hbm, o_ref, buf, sem):
    i = pl.program_id(0)
    rows = o_ref.shape[0]
    slot = i % 2

    @pl.when(i == 0)
    def _():
        _issue_rows(src_hbm, buf.at[0, 0], sem.at[0, 0], p1_ref, rows)
        _issue_rows(src_hbm, buf.at[0, 1], sem.at[0, 1], p2_ref, rows)

    @pl.when(i + 1 < pl.num_programs(0))
    def _():
        _issue_rows(src_hbm, buf.at[1 - slot, 0], sem.at[1 - slot, 0], p1_next_ref, rows)
        _issue_rows(src_hbm, buf.at[1 - slot, 1], sem.at[1 - slot, 1], p2_next_ref, rows)

    _wait_rows(src_hbm, buf.at[slot, 0], sem.at[slot, 0], rows)
    _wait_rows(src_hbm, buf.at[slot, 1], sem.at[slot, 1], rows)
    meta = meta_ref[...]
    o_ref[...] = buf[slot, 0] * meta[:, 4:5] + buf[slot, 1] * meta[:, 5:6]


def combine_rows(yb, pos1, pos2, meta):
    t = pos1.shape[0]
    d = yb.shape[1]
    rows = DMA_ROWS
    assert t % rows == 0
    steps = t // rows
    cur = pl.BlockSpec((1, 1, rows), lambda i: (i, 0, 0), memory_space=pltpu.SMEM)
    nxt = pl.BlockSpec((1, 1, rows), lambda i: (jnp.minimum(i + 1, steps - 1), 0, 0), memory_space=pltpu.SMEM)
    blocks = 3 * _nbytes((rows, d), F32) + _nbytes((rows, LANES), F32)
    p1 = pos1.reshape(steps, 1, rows)
    p2 = pos2.reshape(steps, 1, rows)
    return pl.pallas_call(
        _combine_kernel,
        grid=(steps,),
        in_specs=[cur, cur, nxt, nxt, pl.BlockSpec((rows, LANES), lambda i: (i, 0)),
                  pl.BlockSpec(memory_space=pl.ANY)],
        out_specs=pl.BlockSpec((rows, d), lambda i: (i, 0)),
        out_shape=jax.ShapeDtypeStruct((t, d), F32),
        scratch_shapes=[pltpu.VMEM((2, 2, rows, d), F32), pltpu.SemaphoreType.DMA((2, 2))],
        compiler_params=_params(("arbitrary",), blocks),
        name="moe_combine",
    )(p1, p2, p1, p2, meta, yb)


def _expert_swiglu_kernel(te_ref, tv_ref, tf_ref, x_ref, wg_ref, wu_ref, o_ref, wg_scr, wu_scr):
    i = pl.program_id(1)

    @pl.when(tf_ref[i] > 0)
    def _():
        wg_scr[...] = wg_ref[...].astype(BF16)
        wu_scr[...] = wu_ref[...].astype(BF16)

    @pl.when(tv_ref[i] > 0)
    def _():
        _swiglu_kernel(x_ref, wg_scr, wu_scr, o_ref)

    @pl.when(tv_ref[i] == 0)
    def _():
        o_ref[...] = jnp.zeros(o_ref.shape, o_ref.dtype)


def expert_swiglu_in(xs, w_in, layer, tile_expert, tile_valid, tile_first):
    p, k = xs.shape
    f = w_in.shape[3] // 2
    tn = _ff_tile(f)
    tm = MOE_TILE_ROWS
    nf = f // tn
    blocks = (_nbytes((tm, k), BF16) + 2 * _nbytes((k, tn), F32) + _nbytes((k, tn), BF16)
              + _nbytes((tm, tn), BF16) + _nbytes((tm, tn), F32))
    grid_spec = pltpu.PrefetchScalarGridSpec(
        num_scalar_prefetch=3,
        grid=(nf, p // tm),
        in_specs=[
            pl.BlockSpec((tm, k), lambda j, i, te, tv, tf: (i, 0)),
            pl.BlockSpec((None, None, k, tn), lambda j, i, te, tv, tf: (layer, te[i], 0, j)),
            pl.BlockSpec((None, None, k, tn), lambda j, i, te, tv, tf: (layer, te[i], 0, j + nf)),
        ],
        out_specs=pl.BlockSpec((tm, tn), lambda j, i, te, tv, tf: (i, j)),
        scratch_shapes=[pltpu.VMEM((k, tn), BF16), pltpu.VMEM((k, tn), BF16)],
    )
    return pl.pallas_call(
        _expert_swiglu_kernel,
        grid_spec=grid_spec,
        out_shape=jax.ShapeDtypeStruct((p, f), BF16),
        compiler_params=_params(("arbitrary", "arbitrary"), blocks),
        name="expert_swiglu_in",
    )(tile_expert, tile_valid, tile_first, xs, w_in, w_in)


def _expert_out_kernel(te_ref, tv_ref, h_ref, w_ref, o_ref):
    i = pl.program_id(1)

    @pl.when(tv_ref[i] > 0)
    def _():
        _mm_kernel(h_ref, w_ref, o_ref)

    @pl.when(tv_ref[i] == 0)
    def _():
        o_ref[...] = jnp.zeros(o_ref.shape, o_ref.dtype)


def expert_out(h, w_out, tile_expert, tile_valid):
    p, f = h.shape
    d = w_out.shape[2]
    tm = MOE_TILE_ROWS
    tn = min(1024, d)
    blocks = _nbytes((tm, f), BF16) + _nbytes((f, tn), BF16) + _nbytes((tm, tn), F32) * 2
    grid_spec = pltpu.PrefetchScalarGridSpec(
        num_scalar_prefetch=2,
        grid=(d // tn, p // tm),
        in_specs=[
            pl.BlockSpec((tm, f), lambda j, i, te, tv: (i, 0)),
            pl.BlockSpec((None, f, tn), lambda j, i, te, tv: (te[i], 0, j)),
        ],
        out_specs=pl.BlockSpec((tm, tn), lambda j, i, te, tv: (i, j)),
    )
    return pl.pallas_call(
        _expert_out_kernel,
        grid_spec=grid_spec,
        out_shape=jax.ShapeDtypeStruct((p, d), F32),
        compiler_params=_params(("arbitrary", "arbitrary"), blocks),
        name="expert_out",
    )(tile_expert, tile_valid, h, w_out)


def moe_swiglu(u, w_router, w_in, layer, w_out):
    t, d = u.shape
    tm = MOE_TILE_ROWS
    n_tiles = (t * TOP_K) // tm + N_EXPERTS
    p = n_tiles * tm
    meta, counts = moe_router(u, w_router)
    cnt = counts[0, :N_EXPERTS].astype(jnp.int32)
    tiles_e = (cnt + tm - 1) // tm
    tile_end = jnp.cumsum(tiles_e)
    row_start = (tile_end - tiles_e) * tm
    tile_ids = jnp.arange(n_tiles, dtype=jnp.int32)
    tile_expert = jnp.minimum(jnp.searchsorted(tile_end, tile_ids, side='right'), N_EXPERTS - 1).astype(jnp.int32)
    tile_valid = (tile_ids < tile_end[-1]).astype(jnp.int32)
    tile_first = jnp.concatenate([jnp.ones((1,), jnp.int32),
                                  (tile_expert[1:] != tile_expert[:-1]).astype(jnp.int32)])
    e1 = meta[:, 0].astype(jnp.int32)
    e2 = meta[:, 1].astype(jnp.int32)
    pos1 = row_start[e1] + meta[:, 2].astype(jnp.int32)
    pos2 = row_start[e2] + meta[:, 3].astype(jnp.int32)
    tok = jnp.arange(t, dtype=jnp.int32)
    row_tok = jnp.zeros((p,), jnp.int32).at[pos1].set(tok).at[pos2].set(tok)
    xs = gather_rows(u, row_tok, BF16)
    h = expert_swiglu_in(xs, w_in, layer, tile_expert, tile_valid, tile_first)
    yb = expert_out(h, w_out, tile_expert, tile_valid)
    return combine_rows(yb, pos1, pos2, meta)


def kernel(x, c, ada_w, ada_b, ln_g, ln_b, attn_w_qkv, attn_w_o, rg_w_in, rg_conv_w, rg_conv_b,
           rg_gate_a_w, rg_gate_a_b, rg_gate_x_w, rg_gate_x_b, rg_lambda, rg_w_out,
           ffn_w_in, ffn_w_out, moe_w_router, moe_w_in, moe_w_out):
    b, s, d = x.shape
    depth = ada_w.shape[0]
    t = b * s
    alpha = (2 * depth) ** 0.25
    x = x.astype(F32)

    mod = ada_modulation(c, ada_w, ada_b)
    mod4 = [mod[i].reshape(b, N_MOD, 1, d) for i in range(depth)]
    SH1, SC1, G1, SH2, SC2, G2 = range(N_MOD)

    u = modulate(x, mod4[0], SC1, SH1, BF16)
    for i in range(depth):
        j = i // 2
        is_attn = i % 2 == 0
        if is_attn:
            w_qkv = attn_w_qkv[j].astype(BF16)
            outs, lses = [], []
            for g, (window, dilation) in enumerate(ATTN_GROUPS):
                qkv_g = qkv_projection(u.reshape(t, d), w_qkv, g, dilation)
                o, lse = dilated_attention_group(qkv_g, g, window, dilation, b)
                outs.append(o)
                lses.append(lse)
            merged = merge_groups(outs, lses)
            y = matmul(merged, attn_w_o[j].astype(BF16), F32, 1024, 1024)
        else:
            zr = matmul(u.reshape(t, d), rg_w_in[j].astype(BF16), F32, 1024, 1024).reshape(b, s, -1)
            yr = rglru(zr, rg_conv_w[j], rg_conv_b[j], rg_gate_a_w[j], rg_gate_a_b[j],
                       rg_gate_x_w[j], rg_gate_x_b[j], rg_lambda[j])
            y = matmul(yr.reshape(t, d), rg_w_out[j].astype(BF16), F32, 1024, 1024)
        x, u = residual_layer_norm(x, y, mod4[i], G1, ln_g[i, 0], ln_b[i, 0], alpha,
                                   next_mod4=(mod4[i], SC2, SH2), u_dtype=BF16 if is_attn else F32)
        if is_attn:
            h = swiglu_in(u.reshape(t, d), ffn_w_in[j].astype(BF16), 1024)
            y = matmul(h, ffn_w_out[j].astype(BF16), F32, 512, 1024)
        else:
            y = moe_swiglu(u.reshape(t, d), moe_w_router[j], moe_w_in, j, moe_w_out[j].astype(BF16))
        nxt = (mod4[i + 1], SC1, SH1) if i + 1 < depth else None
        x, u = residual_layer_norm(x, y, mod4[i], G2, ln_g[i, 1], ln_b[i, 1], alpha, next_mod4=nxt, u_dtype=BF16)
    return x
```

```python
import functools

import jax
import jax.numpy as jnp
from jax import lax
from jax.experimental import pallas as pl
from jax.experimental.pallas import tpu as pltpu

F32 = jnp.float32
BF16 = jnp.bfloat16

N_HEADS = 16
ATTN_GROUPS = ((128, 1), (512, 4), (2048, 16))
Q_BLOCK = 128
RNN_BLOCK = 256
CONV_WIDTH = 4
LRU_C = 8.0
N_EXPERTS = 8
TOP_K = 2
LN_EPS = 1e-5
NEG_INF = -1e30
N_MOD = 6

LANES = 128
SUBLANES = 8
VMEM_BYTES = 64 * 1024 * 1024

MOE_TILE_ROWS = 1024
MOE_OUT_ROWS = 512
DMA_ROWS = 256


def _vmem_limit(block_bytes):
    want = 2 * block_bytes + 12 * 1024 * 1024
    return int(min(want, VMEM_BYTES - 6 * 1024 * 1024))


def _params(semantics, block_bytes):
    return pltpu.CompilerParams(dimension_semantics=semantics,
                                vmem_limit_bytes=_vmem_limit(block_bytes))


def _nbytes(shape, dtype):
    n = 1
    for s in shape:
        n *= s
    return n * jnp.dtype(dtype).itemsize


def _ada_kernel(c_ref, w_ref, b_ref, o_ref):
    c = c_ref[...]
    c_act = (c * jax.nn.sigmoid(c)).astype(BF16)
    o_ref[...] = jnp.dot(c_act, w_ref[...].astype(BF16), preferred_element_type=F32) + b_ref[...]


def ada_modulation(c, ada_w, ada_b):
    depth, d, n = ada_w.shape
    b = c.shape[0]
    tn = 1536 if n % 1536 == 0 else n
    blocks = _nbytes((d, tn), F32) + _nbytes((b, tn), F32) * 2
    return pl.pallas_call(
        _ada_kernel,
        grid=(depth, n // tn),
        in_specs=[
            pl.BlockSpec((b, d), lambda i, j: (0, 0)),
            pl.BlockSpec((None, d, tn), lambda i, j: (i, 0, j)),
            pl.BlockSpec((None, 1, tn), lambda i, j: (i, 0, j)),
        ],
        out_specs=pl.BlockSpec((None, b, tn), lambda i, j: (i, 0, j)),
        out_shape=jax.ShapeDtypeStruct((depth, b, n), F32),
        compiler_params=_params(("arbitrary", "arbitrary"), blocks),
        name="ada_modulation",
    )(c, ada_w, ada_b.reshape(depth, 1, n))


def _store_u(u, u_refs, slab_scr):
    tm, d = u.shape
    if any(ref.shape[0] > 1 for ref in u_refs):
        for c in range(d // LANES):
            slab_scr[c] = u[:, c * LANES:(c + 1) * LANES]
    for ref in u_refs:
        dl = ref.shape[0]
        if dl == 1:
            ref[0] = u.astype(ref.dtype)
            continue
        for r in range(dl):
            for c in range(d // LANES):
                piece = slab_scr[c, pl.ds(r, tm // dl, stride=dl), :]
                ref[r, :, c * LANES:(c + 1) * LANES] = piece.astype(ref.dtype)


def _modulate_kernel(x_ref, sc_ref, sh_ref, *rest, n_u):
    u_refs, scr = rest[:n_u], rest[n_u:]
    u = x_ref[...] * (1.0 + sc_ref[...]) + sh_ref[...]
    _store_u(u, u_refs, scr[0] if scr else None)


def _ln_kernel(x_ref, y_ref, gate_ref, lng_ref, lnb_ref, *rest, alpha, n_u):
    if n_u:
        sc_ref, sh_ref, xo_ref = rest[:3]
        u_refs, scr = rest[3:3 + n_u], rest[3 + n_u:]
    else:
        (xo_ref,) = rest
    z = alpha * x_ref[...] + (1.0 + gate_ref[...]) * y_ref[...].astype(F32)
    mu = jnp.mean(z, axis=-1, keepdims=True)
    zc = z - mu
    var = jnp.mean(zc * zc, axis=-1, keepdims=True)
    xn = zc * lax.rsqrt(var + LN_EPS) * lng_ref[...] + lnb_ref[...]
    xo_ref[...] = xn
    if n_u:
        _store_u(xn * (1.0 + sc_ref[...]) + sh_ref[...], u_refs, scr[0] if scr else None)


def _row_tile(s):
    for t in (512, 256, 128, 64, 32, 16, 8):
        if s % t == 0:
            return t
    return s


def _u_outputs(b, s, d, tm, dilations, u_dtype):
    t = b * s
    assert all(tm % (dl * 2 * SUBLANES) == 0 or dl == 1 for dl in dilations)
    specs = [pl.BlockSpec((dl, tm // dl, d), lambda bi, ti: (0, bi * (s // tm) + ti, 0)) for dl in dilations]
    shapes = [jax.ShapeDtypeStruct((dl, t // dl, d), u_dtype) for dl in dilations]
    scratch = [pltpu.VMEM((d // LANES, tm, LANES), F32)] if any(dl > 1 for dl in dilations) else []
    return specs, shapes, scratch


def modulate(x, mod4, sc_chunk, sh_chunk, u_dtype, dilations=(1,)):
    b, s, d = x.shape
    tm = _row_tile(s)
    vec = lambda ch: pl.BlockSpec((None, None, 1, d), lambda bi, ti, ch=ch: (bi, ch, 0, 0))
    row = pl.BlockSpec((None, tm, d), lambda bi, ti: (bi, ti, 0))
    u_specs, u_shapes, scratch = _u_outputs(b, s, d, tm, dilations, u_dtype)
    blocks = _nbytes((tm, d), F32) * (2 + len(dilations))
    return pl.pallas_call(
        functools.partial(_modulate_kernel, n_u=len(dilations)),
        grid=(b, s // tm),
        in_specs=[row, vec(sc_chunk), vec(sh_chunk)],
        out_specs=u_specs,
        out_shape=u_shapes,
        scratch_shapes=scratch,
        compiler_params=_params(("arbitrary", "arbitrary"), blocks),
        name="modulate",
    )(x, mod4, mod4)


def residual_layer_norm(x, y, mod4, gate_chunk, ln_g, ln_b, alpha, next_mod4=None, u_dtype=BF16, dilations=(1,)):
    b, s, d = x.shape
    tm = _row_tile(s)
    with_u = next_mod4 is not None
    vec = lambda ch: pl.BlockSpec((None, None, 1, d), lambda bi, ti, ch=ch: (bi, ch, 0, 0))
    row = pl.BlockSpec((None, tm, d), lambda bi, ti: (bi, ti, 0))
    par = pl.BlockSpec((1, d), lambda bi, ti: (0, 0))
    in_specs = [row, row, vec(gate_chunk), par, par]
    args = [x, y.reshape(b, s, d), mod4, ln_g.reshape(1, d), ln_b.reshape(1, d)]
    out_specs = [row]
    out_shape = [jax.ShapeDtypeStruct((b, s, d), F32)]
    scratch = []
    n_u = 0
    if with_u:
        nm4, sc_chunk, sh_chunk = next_mod4
        in_specs += [vec(sc_chunk), vec(sh_chunk)]
        args += [nm4, nm4]
        u_specs, u_shapes, scratch = _u_outputs(b, s, d, tm, dilations, u_dtype)
        out_specs += u_specs
        out_shape += u_shapes
        n_u = len(dilations)
    out = pl.pallas_call(
        functools.partial(_ln_kernel, alpha=alpha, n_u=n_u),
        grid=(b, s // tm),
        in_specs=in_specs,
        out_specs=out_specs,
        out_shape=out_shape,
        scratch_shapes=scratch,
        compiler_params=_params(("arbitrary", "arbitrary"), _nbytes((tm, d), F32) * (4 + n_u)),
        name="residual_layer_norm",
    )(*args)
    return out[0], list(out[1:])


def _mm_kernel(x_ref, w_ref, o_ref):
    o_ref[...] = jnp.dot(x_ref[...], w_ref[...], preferred_element_type=F32).astype(o_ref.dtype)


def matmul(x, w, out_dtype, tm, tn):
    m, k = x.shape
    n = w.shape[1]
    tm = min(tm, m)
    tn = min(tn, n)
    blocks = _nbytes((tm, k), BF16) + _nbytes((k, tn), BF16) + _nbytes((tm, tn), out_dtype) + _nbytes((tm, tn), F32) // 2
    return pl.pallas_call(
        _mm_kernel,
        grid=(n // tn, m // tm),
        in_specs=[
            pl.BlockSpec((tm, k), lambda j, i: (i, 0)),
            pl.BlockSpec((k, tn), lambda j, i: (0, j)),
        ],
        out_specs=pl.BlockSpec((tm, tn), lambda j, i: (i, j)),
        out_shape=jax.ShapeDtypeStruct((m, n), out_dtype),
        compiler_params=_params(("arbitrary", "arbitrary"), blocks),
        name="matmul",
    )(x, w)


def _swiglu_kernel(x_ref, wg_ref, wu_ref, o_ref):
    x = x_ref[...]
    g = jnp.dot(x, wg_ref[...], preferred_element_type=F32)
    up = jnp.dot(x, wu_ref[...], preferred_element_type=F32)
    o_ref[...] = (g * jax.nn.sigmoid(g) * up).astype(o_ref.dtype)


def _ff_tile(f):
    for t in (512, 256, 128):
        if f % t == 0:
            return t
    return f


def swiglu_in(x, w_in, tm):
    m, k = x.shape
    f = w_in.shape[1] // 2
    tn = _ff_tile(f)
    tm = min(tm, m)
    nf = f // tn
    blocks = _nbytes((tm, k), BF16) + 2 * _nbytes((k, tn), BF16) + _nbytes((tm, tn), BF16) + _nbytes((tm, tn), F32)
    return pl.pallas_call(
        _swiglu_kernel,
        grid=(nf, m // tm),
        in_specs=[
            pl.BlockSpec((tm, k), lambda j, i: (i, 0)),
            pl.BlockSpec((k, tn), lambda j, i: (0, j)),
            pl.BlockSpec((k, tn), lambda j, i: (0, j + nf)),
        ],
        out_specs=pl.BlockSpec((tm, tn), lambda j, i: (i, j)),
        out_shape=jax.ShapeDtypeStruct((m, f), BF16),
        compiler_params=_params(("arbitrary", "arbitrary"), blocks),
        name="swiglu_in",
    )(x, w_in, w_in)


def qkv_projection(u, w_qkv, g):
    t, d = u.shape
    tm = min(1024, t)
    assert t % tm == 0 and d % LANES == 0
    blocks = _nbytes((tm, d), BF16) * 2 + _nbytes((d, d), BF16) + _nbytes((tm, d), F32)
    return pl.pallas_call(
        _mm_kernel,
        grid=(3, t // tm),
        in_specs=[
            pl.BlockSpec((tm, d), lambda j, i: (i, 0)),
            pl.BlockSpec((d, d), lambda j, i: (0, g * 3 + j)),
        ],
        out_specs=pl.BlockSpec((None, tm, d), lambda j, i: (j, i, 0)),
        out_shape=jax.ShapeDtypeStruct((3, t, d), BF16),
        compiler_params=_params(("arbitrary", "arbitrary"), blocks),
        name=f"qkv_projection_g{g}",
    )(u, w_qkv)


def _attn_kernel(q_ref, kp_ref, kc_ref, vp_ref, vc_ref, o_ref, lse_ref, s_scr, p_scr, m_scr, linv_scr,
                 *, dilation, steps, head_dim):
    n = pl.program_id(2)
    qb = q_ref.shape[0]
    qi = lax.broadcasted_iota(jnp.int32, (qb, 2 * qb), 0)
    ki = lax.broadcasted_iota(jnp.int32, (qb, 2 * qb), 1)
    rel = qi + qb - ki
    no_prev = jnp.where(n > 0, 0, 2 * steps + 2 * qb)
    rel_m = jnp.where(ki < qb, rel + no_prev, rel)
    valid = (rel_m >= 0) & (rel_m <= steps)
    base = -(rel * dilation).astype(F32)
    scale = head_dim ** -0.5
    contract_last = (((1,), (1,)), ((), ()))
    heads = [slice(h * head_dim, (h + 1) * head_dim) for h in range(N_HEADS)]

    for h, hs in enumerate(heads):
        slope = 2.0 ** (-8.0 * (h + 1) / N_HEADS)
        k = jnp.concatenate([kp_ref[:, hs], kc_ref[:, hs]], axis=0)
        s = lax.dot_general(q_ref[:, hs], k, contract_last, preferred_element_type=F32)
        s = jnp.where(valid, s * scale + slope * base, NEG_INF)
        s_scr[h] = s
        m_scr[h] = jnp.broadcast_to(jnp.max(s, axis=-1, keepdims=True), m_scr.shape[1:])
    lane = lax.broadcasted_iota(jnp.int32, lse_ref.shape, 1)
    lse_tile = jnp.zeros(lse_ref.shape, F32)
    for h in range(N_HEADS):
        m = m_scr[h]
        p = jnp.exp(s_scr[h] - jnp.concatenate([m, m], axis=1))
        l = jnp.sum(p, axis=-1, keepdims=True)
        p_scr[h] = p.astype(BF16)
        linv_scr[h] = jnp.broadcast_to(1.0 / l, linv_scr.shape[1:])
        lse_tile = jnp.where(lane == h, m + jnp.log(l), lse_tile)
    lse_ref[...] = lse_tile
    for h, hs in enumerate(heads):
        v = jnp.concatenate([vp_ref[:, hs], vc_ref[:, hs]], axis=0)
        o = jnp.dot(p_scr[h], v, preferred_element_type=F32)
        o_ref[:, hs] = (o * linv_scr[h]).astype(o_ref.dtype)


def dilated_attention_group(qkv_g, g, window, dilation, batch):
    _, t, d = qkv_g.shape
    rows = t // dilation
    head_dim = d // N_HEADS
    steps = window // dilation
    l = rows // batch
    assert steps <= Q_BLOCK and l % Q_BLOCK == 0 and N_HEADS <= LANES and head_dim == LANES
    nb = l // Q_BLOCK
    rb = rows // Q_BLOCK
    cur = lambda j: (lambda bi, r, n: (j, r * rb + bi * nb + n, 0))
    prev = lambda j: (lambda bi, r, n: (j, r * rb + bi * nb + jnp.maximum(n - 1, 0), 0))
    spec = lambda im: pl.BlockSpec((None, Q_BLOCK, d), im)
    blocks = 6 * _nbytes((Q_BLOCK, d), BF16) + _nbytes((Q_BLOCK, LANES), F32)
    scratch = (_nbytes((N_HEADS, Q_BLOCK, 2 * Q_BLOCK), F32) + _nbytes((N_HEADS, Q_BLOCK, 2 * Q_BLOCK), BF16)
               + 2 * _nbytes((N_HEADS, Q_BLOCK, LANES), F32))
    return pl.pallas_call(
        functools.partial(_attn_kernel, dilation=dilation, steps=steps, head_dim=head_dim),
        grid=(batch, dilation, nb),
        in_specs=[spec(cur(0)), spec(prev(1)), spec(cur(1)), spec(prev(2)), spec(cur(2))],
        out_specs=[
            pl.BlockSpec((None, Q_BLOCK, d), lambda bi, r, n: (r, bi * nb + n, 0)),
            pl.BlockSpec((None, Q_BLOCK, LANES), lambda bi, r, n: (r, bi * nb + n, 0)),
        ],
        out_shape=[
            jax.ShapeDtypeStruct((dilation, rows, d), BF16),
            jax.ShapeDtypeStruct((dilation, rows, LANES), F32),
        ],
        scratch_shapes=[
            pltpu.VMEM((N_HEADS, Q_BLOCK, 2 * Q_BLOCK), F32),
            pltpu.VMEM((N_HEADS, Q_BLOCK, 2 * Q_BLOCK), BF16),
            pltpu.VMEM((N_HEADS, Q_BLOCK, LANES), F32),
            pltpu.VMEM((N_HEADS, Q_BLOCK, LANES), F32),
        ],
        compiler_params=_params(("arbitrary", "arbitrary", "arbitrary"), blocks + scratch),
        name=f"dilated_attention_g{g}",
    )(qkv_g, qkv_g, qkv_g, qkv_g, qkv_g)


def _merge_kernel(o0_ref, o1_ref, o2_ref, l0_ref, l1_ref, l2_ref, out_ref, nat1_scr, nat2_scr, lse1_scr, lse2_scr,
                  *, head_dim):
    tm = out_ref.shape[0]
    for o_ref, l_ref, nat_scr, lse_scr in ((o1_ref, l1_ref, nat1_scr, lse1_scr), (o2_ref, l2_ref, nat2_scr, lse2_scr)):
        dilation = o_ref.shape[0]
        rows = tm // dilation
        for r in range(dilation):
            lse_scr[pl.ds(r, rows, stride=dilation), :] = l_ref[r]
            for h in range(N_HEADS):
                nat_scr[h, pl.ds(r, rows, stride=dilation), :] = o_ref[r, :, h * head_dim:(h + 1) * head_dim].astype(F32)
    l0, l1, l2 = l0_ref[0], lse1_scr[...], lse2_scr[...]
    m = jnp.maximum(jnp.maximum(l0, l1), l2)
    e0, e1, e2 = jnp.exp(l0 - m), jnp.exp(l1 - m), jnp.exp(l2 - m)
    den = e0 + e1 + e2
    w0, w1, w2 = e0 / den, e1 / den, e2 / den
    for h in range(N_HEADS):
        hs = slice(h * head_dim, (h + 1) * head_dim)
        acc = o0_ref[0, :, hs].astype(F32) * w0[:, h:h + 1]
        acc = acc + nat1_scr[h] * w1[:, h:h + 1]
        acc = acc + nat2_scr[h] * w2[:, h:h + 1]
        out_ref[:, hs] = acc.astype(out_ref.dtype)


def merge_groups(outs, lses):
    dils = [o.shape[0] for o in outs]
    assert dils[0] == 1 and len(outs) == 3
    t, d = outs[0].shape[1:]
    head_dim = d // N_HEADS
    tm = 512
    assert t % tm == 0 and all(tm % (dl * 2 * SUBLANES) == 0 for dl in dils) and head_dim == LANES
    ospec = lambda dl: pl.BlockSpec((dl, tm // dl, d), lambda i: (0, i, 0))
    lspec = lambda dl: pl.BlockSpec((dl, tm // dl, LANES), lambda i: (0, i, 0))
    blocks = 4 * _nbytes((tm, d), BF16) + 3 * _nbytes((tm, LANES), F32) + 3 * _nbytes((tm, d), F32)
    return pl.pallas_call(
        functools.partial(_merge_kernel, head_dim=head_dim),
        grid=(t // tm,),
        in_specs=[ospec(dl) for dl in dils] + [lspec(dl) for dl in dils],
        out_specs=pl.BlockSpec((tm, d), lambda i: (i, 0)),
        out_shape=jax.ShapeDtypeStruct((t, d), BF16),
        scratch_shapes=[
            pltpu.VMEM((N_HEADS, tm, LANES), F32),
            pltpu.VMEM((N_HEADS, tm, LANES), F32),
            pltpu.VMEM((tm, LANES), F32),
            pltpu.VMEM((tm, LANES), F32),
        ],
        compiler_params=_params(("arbitrary",), blocks),
        name="merge_groups",
    )(*outs, *lses)


def _softplus(x):
    return jnp.maximum(x, 0.0) + jnp.log1p(jnp.exp(-jnp.abs(x)))


def _rglru_kernel(gate_ref, rec_ref, cw_ref, cb_ref, gaw_ref, gab_ref, gxw_ref, gxb_ref, lam_ref,
                  y_ref, ext_scr, a_scr, b_scr, h_scr, carry_scr):
    t = pl.program_id(1)
    nb, ts, cb = rec_ref.shape
    halo = SUBLANES
    n_half = cb // LANES

    @pl.when(t == 0)
    def _():
        ext_scr[:, 0:halo, :] = jnp.zeros((nb, halo, cb), F32)
        carry_scr[...] = jnp.zeros(carry_scr.shape, F32)

    rec = rec_ref[...]
    ext_scr[:, halo:halo + ts, :] = rec
    xc = cb_ref[...].reshape(1, 1, cb)
    for k in range(CONV_WIDTH):
        off = halo - (CONV_WIDTH - 1) + k
        xc = xc + cw_ref[k:k + 1, :].reshape(1, 1, cb) * ext_scr[:, off:off + ts, :]
    ext_scr[:, 0:halo, :] = rec[:, ts - halo:ts, :]

    xc2 = xc.reshape(nb * ts, cb)
    xb = xc2.astype(BF16)
    r = jax.nn.sigmoid(jnp.dot(xb, gaw_ref[...], preferred_element_type=F32) + gab_ref[...])
    i = jax.nn.sigmoid(jnp.dot(xb, gxw_ref[...], preferred_element_type=F32) + gxb_ref[...])
    log_a = -LRU_C * r * _softplus(-lam_ref[...])
    a = jnp.exp(log_a)
    bb = jnp.sqrt(-jnp.tanh(log_a) * (a * a + 1.0)) * (i * xc2)
    pitch = a_scr.shape[1] // nb
    for c in range(n_half):
        for bi in range(nb):
            a_scr[c, bi * pitch:bi * pitch + ts, :] = a[bi * ts:(bi + 1) * ts, c * LANES:(c + 1) * LANES]
            b_scr[c, bi * pitch:bi * pitch + ts, :] = bb[bi * ts:(bi + 1) * ts, c * LANES:(c + 1) * LANES]

    def step(j, hs):
        out = []
        for c in range(n_half):
            a_t = a_scr[c, pl.ds(j, nb, stride=pitch), :]
            b_t = b_scr[c, pl.ds(j, nb, stride=pitch), :]
            h = a_t * hs[c] + b_t
            h_scr[c, pl.ds(j, nb, stride=pitch), :] = h
            out.append(h)
        return tuple(out)

    h0 = tuple(carry_scr[c] for c in range(n_half))
    hn = lax.fori_loop(0, ts, step, h0, unroll=8)
    for c in range(n_half):
        carry_scr[c] = hn[c]

    gate = jax.nn.gelu(gate_ref[...])
    for c in range(n_half):
        cs = slice(c * LANES, (c + 1) * LANES)
        for bi in range(nb):
            y_ref[bi, :, cs] = (gate[bi, :, cs] * h_scr[c, bi * pitch:bi * pitch + ts, :]).astype(y_ref.dtype)


def rglru(zr, conv_w, conv_b, ga_w, ga_b, gx_w, gx_b, lam):
    b, s, c2 = zr.shape
    d = c2 // 2
    cb = RNN_BLOCK
    ncb = d // cb
    ts = min(256, s)
    assert s % ts == 0 and ts % SUBLANES == 0 and cb % LANES == 0
    n_half = cb // LANES
    vec = pl.BlockSpec((1, cb), lambda n, t: (0, n))
    wspec = pl.BlockSpec((None, cb, cb), lambda n, t: (n, 0, 0))
    blocks = 2 * _nbytes((b, ts, cb), F32) + _nbytes((b, ts, cb), BF16) + 2 * _nbytes((cb, cb), BF16)
    scratch = 4 * _nbytes((b, ts + SUBLANES, cb), F32)
    return pl.pallas_call(
        _rglru_kernel,
        grid=(ncb, s // ts),
        in_specs=[
            pl.BlockSpec((b, ts, cb), lambda n, t: (0, t, n)),
            pl.BlockSpec((b, ts, cb), lambda n, t: (0, t, n + ncb)),
            pl.BlockSpec((CONV_WIDTH, cb), lambda n, t: (0, n)),
            vec, wspec, vec, wspec, vec, vec,
        ],
        out_specs=pl.BlockSpec((b, ts, cb), lambda n, t: (0, t, n)),
        out_shape=jax.ShapeDtypeStruct((b, s, d), BF16),
        scratch_shapes=[
            pltpu.VMEM((b, ts + SUBLANES, cb), F32),
            pltpu.VMEM((n_half, b * (ts + SUBLANES), LANES), F32),
            pltpu.VMEM((n_half, b * (ts + SUBLANES), LANES), F32),
            pltpu.VMEM((n_half, b * (ts + SUBLANES), LANES), F32),
            pltpu.VMEM((n_half, b, LANES), F32),
        ],
        compiler_params=_params(("arbitrary", "arbitrary"), blocks + scratch // 2 + 8 * _nbytes((b, ts, cb), F32)),
        name="rglru",
    )(zr, zr, conv_w, conv_b.reshape(1, d), ga_w.astype(BF16), ga_b.reshape(1, d),
      gx_w.astype(BF16), gx_b.reshape(1, d), lam.reshape(1, d))


def _router_kernel(u_ref, wr_ref, meta_ref, cnt_ref, carry_scr):
    step = pl.program_id(0)

    @pl.when(step == 0)
    def _():
        carry_scr[...] = jnp.zeros(carry_scr.shape, F32)

    tr = u_ref.shape[0]
    logits = jnp.dot(u_ref[...], wr_ref[...], preferred_element_type=F32, precision=lax.Precision.HIGHEST)
    lane = lax.broadcasted_iota(jnp.int32, logits.shape, 1).astype(F32)
    logits = jnp.where(lane < N_EXPERTS, logits, -jnp.inf)
    m1 = jnp.max(logits, axis=-1, keepdims=True)
    i1 = jnp.min(jnp.where(logits == m1, lane, float(LANES)), axis=-1, keepdims=True)
    rest = jnp.where(lane == i1, -jnp.inf, logits)
    m2 = jnp.max(rest, axis=-1, keepdims=True)
    i2 = jnp.min(jnp.where(rest == m2, lane, float(LANES)), axis=-1, keepdims=True)
    e2 = jnp.exp(m2 - m1)
    den = 1.0 + e2
    w1 = 1.0 / den
    w2 = e2 / den
    hit = ((lane == i1) | (lane == i2))
    rows = lax.broadcasted_iota(jnp.int32, (tr, tr), 0)
    cols = lax.broadcasted_iota(jnp.int32, (tr, tr), 1)
    strict_lower = (cols < rows).astype(BF16)
    before = jnp.dot(strict_lower, hit.astype(BF16), preferred_element_type=F32) + carry_scr[...]
    r1 = jnp.sum(jnp.where(lane == i1, before, 0.0), axis=-1, keepdims=True)
    r2 = jnp.sum(jnp.where(lane == i2, before, 0.0), axis=-1, keepdims=True)
    total = carry_scr[...] + jnp.sum(hit.astype(F32), axis=0, keepdims=True)
    carry_scr[...] = total
    cnt_ref[...] = total
    meta = jnp.where(lane == 0, i1, 0.0)
    meta = jnp.where(lane == 1, i2, meta)
    meta = jnp.where(lane == 2, r1, meta)
    meta = jnp.where(lane == 3, r2, meta)
    meta = jnp.where(lane == 4, w1, meta)
    meta = jnp.where(lane == 5, w2, meta)
    meta_ref[...] = meta


def moe_router(u, w_router):
    t, d = u.shape
    tr = _row_tile(t)
    wr = jnp.zeros((d, LANES), F32).at[:, :N_EXPERTS].set(w_router)
    blocks = _nbytes((tr, d), F32) * 2 + _nbytes((tr, tr), F32)
    return pl.pallas_call(
        _router_kernel,
        grid=(t // tr,),
        in_specs=[pl.BlockSpec((tr, d), lambda i: (i, 0)), pl.BlockSpec((d, LANES), lambda i: (0, 0))],
        out_specs=[pl.BlockSpec((tr, LANES), lambda i: (i, 0)), pl.BlockSpec((1, LANES), lambda i: (0, 0))],
        out_shape=[jax.ShapeDtypeStruct((t, LANES), F32), jax.ShapeDtypeStruct((1, LANES), F32)],
        scratch_shapes=[pltpu.VMEM((1, LANES), F32)],
        compiler_params=_params(("arbitrary",), blocks),
        name="moe_router",
    )(u, wr)


def _row_copy(src_hbm, dst_vmem, sem, src_row, dst_row):
    return pltpu.make_async_copy(src_hbm.at[pl.ds(src_row, 1), :], dst_vmem.at[pl.ds(dst_row, 1), :], sem)


def _issue_rows(src_hbm, dst_vmem, sem, idx_ref, rows):
    def body(r, carry):
        _row_copy(src_hbm, dst_vmem, sem, idx_ref[0, 0, r], r).start()
        return carry

    lax.fori_loop(0, rows, body, 0, unroll=8)


def _wait_rows(src_hbm, dst_vmem, sem, rows):
    pltpu.make_async_copy(src_hbm.at[pl.ds(0, rows), :], dst_vmem, sem).wait()


def _gather_kernel(idx_ref, idx_next_ref, src_hbm, o_ref, buf, sem):
    i = pl.program_id(0)
    rows = o_ref.shape[0]
    slot = i % 2

    @pl.when(i == 0)
    def _():
        _issue_rows(src_hbm, buf.at[0], sem.at[0], idx_ref, rows)

    @pl.when(i + 1 < pl.num_programs(0))
    def _():
        _issue_rows(src_hbm, buf.at[1 - slot], sem.at[1 - slot], idx_next_ref, rows)

    _wait_rows(src_hbm, buf.at[slot], sem.at[slot], rows)
    o_ref[...] = buf[slot].astype(o_ref.dtype)


def gather_rows(src, idx, out_dtype):
    p = idx.shape[0]
    d = src.shape[1]
    rows = DMA_ROWS
    assert p % rows == 0
    steps = p // rows
    blocks = _nbytes((rows, d), src.dtype) + _nbytes((rows, d), out_dtype)
    idx3 = idx.reshape(steps, 1, rows)
    return pl.pallas_call(
        _gather_kernel,
        grid=(steps,),
        in_specs=[
            pl.BlockSpec((1, 1, rows), lambda i: (i, 0, 0), memory_space=pltpu.SMEM),
            pl.BlockSpec((1, 1, rows), lambda i: (jnp.minimum(i + 1, steps - 1), 0, 0), memory_space=pltpu.SMEM),
            pl.BlockSpec(memory_space=pl.ANY),
        ],
        out_specs=pl.BlockSpec((rows, d), lambda i: (i, 0)),
        out_shape=jax.ShapeDtypeStruct((p, d), out_dtype),
        scratch_shapes=[pltpu.VMEM((2, rows, d), src.dtype), pltpu.SemaphoreType.DMA((2,))],
        compiler_params=_params(("arbitrary",), blocks),
        name="moe_gather",
    )(idx3, idx3, src)


def _combine_kernel(p1_ref, p2_ref, p1_next_ref, p2_next_ref, meta_ref, src_hbm, o_ref, buf, sem):
    i = pl.program_id(0)
    rows = o_ref.shape[0]
    slot = i % 2

    @pl.when(i == 0)
    def _():
        _issue_rows(src_hbm, buf.at[0, 0], sem.at[0, 0], p1_ref, rows)
        _issue_rows(src_hbm, buf.at[0, 1], sem.at[0, 1], p2_ref, rows)

    @pl.when(i + 1 < pl.num_programs(0))
    def _():
        _issue_rows(src_hbm, buf.at[1 - slot, 0], sem.at[1 - slot, 0], p1_next_ref, rows)
        _issue_rows(src_hbm, buf.at[1 - slot, 1], sem.at[1 - slot, 1], p2_next_ref, rows)

    _wait_rows(src_hbm, buf.at[slot, 0], sem.at[slot, 0], rows)
    _wait_rows(src_hbm, buf.at[slot, 1], sem.at[slot, 1], rows)
    meta = meta_ref[...]
    o_ref[...] = buf[slot, 0] * meta[:, 4:5] + buf[slot, 1] * meta[:, 5:6]


def combine_rows(yb, pos1, pos2, meta):
    t = pos1.shape[0]
    d = yb.shape[1]
    rows = DMA_ROWS
    assert t % rows == 0
    steps = t // rows
    cur = pl.BlockSpec((1, 1, rows), lambda i: (i, 0, 0), memory_space=pltpu.SMEM)
    nxt = pl.BlockSpec((1, 1, rows), lambda i: (jnp.minimum(i + 1, steps - 1), 0, 0), memory_space=pltpu.SMEM)
    blocks = 3 * _nbytes((rows, d), F32) + _nbytes((rows, LANES), F32)
    p1 = pos1.reshape(steps, 1, rows)
    p2 = pos2.reshape(steps, 1, rows)
    return pl.pallas_call(
        _combine_kernel,
        grid=(steps,),
        in_specs=[cur, cur, nxt, nxt, pl.BlockSpec((rows, LANES), lambda i: (i, 0)),
                  pl.BlockSpec(memory_space=pl.ANY)],
        out_specs=pl.BlockSpec((rows, d), lambda i: (i, 0)),
        out_shape=jax.ShapeDtypeStruct((t, d), F32),
        scratch_shapes=[pltpu.VMEM((2, 2, rows, d), F32), pltpu.SemaphoreType.DMA((2, 2))],
        compiler_params=_params(("arbitrary",), blocks),
        name="moe_combine",
    )(p1, p2, p1, p2, meta, yb)


def _expert_swiglu_kernel(te_ref, tv_ref, tf_ref, x_ref, wg_ref, wu_ref, o_ref, wg_scr, wu_scr):
    i = pl.program_id(1)

    @pl.when(tf_ref[i] > 0)
    def _():
        wg_scr[...] = wg_ref[...].astype(BF16)
        wu_scr[...] = wu_ref[...].astype(BF16)

    @pl.when(tv_ref[i] > 0)
    def _():
        _swiglu_kernel(x_ref, wg_scr, wu_scr, o_ref)

    @pl.when(tv_ref[i] == 0)
    def _():
        o_ref[...] = jnp.zeros(o_ref.shape, o_ref.dtype)


def expert_swiglu_in(xs, w_in, layer, tile_expert, tile_valid, tile_first):
    p, k = xs.shape
    f = w_in.shape[3] // 2
    tn = _ff_tile(f)
    tm = MOE_TILE_ROWS
    nf = f // tn
    blocks = (_nbytes((tm, k), BF16) + 2 * _nbytes((k, tn), F32) + _nbytes((k, tn), BF16)
              + _nbytes((tm, tn), BF16) + _nbytes((tm, tn), F32))
    grid_spec = pltpu.PrefetchScalarGridSpec(
        num_scalar_prefetch=3,
        grid=(nf, p // tm),
        in_specs=[
            pl.BlockSpec((tm, k), lambda j, i, te, tv, tf: (i, 0)),
            pl.BlockSpec((None, None, k, tn), lambda j, i, te, tv, tf: (layer, te[i], 0, j)),
            pl.BlockSpec((None, None, k, tn), lambda j, i, te, tv, tf: (layer, te[i], 0, j + nf)),
        ],
        out_specs=pl.BlockSpec((tm, tn), lambda j, i, te, tv, tf: (i, j)),
        scratch_shapes=[pltpu.VMEM((k, tn), BF16), pltpu.VMEM((k, tn), BF16)],
    )
    return pl.pallas_call(
        _expert_swiglu_kernel,
        grid_spec=grid_spec,
        out_shape=jax.ShapeDtypeStruct((p, f), BF16),
        compiler_params=_params(("arbitrary", "arbitrary"), blocks),
        name="expert_swiglu_in",
    )(tile_expert, tile_valid, tile_first, xs, w_in, w_in)


def _expert_out_kernel(te_ref, tv_ref, tf_ref, h_ref, w_ref, o_ref, w_scr):
    i = pl.program_id(1)

    @pl.when(tf_ref[i] > 0)
    def _():
        w_scr[...] = w_ref[...].astype(BF16)

    @pl.when(tv_ref[i] > 0)
    def _():
        _mm_kernel(h_ref, w_scr, o_ref)

    @pl.when(tv_ref[i] == 0)
    def _():
        o_ref[...] = jnp.zeros(o_ref.shape, o_ref.dtype)


def expert_out(h, w_out, layer, sub_expert, sub_valid, sub_first):
    p, f = h.shape
    d = w_out.shape[3]
    tm = MOE_OUT_ROWS
    tn = min(512, d)
    blocks = _nbytes((tm, f), BF16) + _nbytes((f, tn), F32) + _nbytes((f, tn), BF16) // 2 + _nbytes((tm, tn), F32) * 2
    grid_spec = pltpu.PrefetchScalarGridSpec(
        num_scalar_prefetch=3,
        grid=(d // tn, p // tm),
        in_specs=[
            pl.BlockSpec((tm, f), lambda j, i, te, tv, tf: (i, 0)),
            pl.BlockSpec((None, None, f, tn), lambda j, i, te, tv, tf: (layer, te[i], 0, j)),
        ],
        out_specs=pl.BlockSpec((tm, tn), lambda j, i, te, tv, tf: (i, j)),
        scratch_shapes=[pltpu.VMEM((f, tn), BF16)],
    )
    return pl.pallas_call(
        _expert_out_kernel,
        grid_spec=grid_spec,
        out_shape=jax.ShapeDtypeStruct((p, d), F32),
        compiler_params=_params(("arbitrary", "arbitrary"), blocks),
        name="expert_out",
    )(sub_expert, sub_valid, sub_first, h, w_out)


def _first_flags(ids):
    return jnp.concatenate([jnp.ones((1,), jnp.int32), (ids[1:] != ids[:-1]).astype(jnp.int32)])


def moe_swiglu(u, w_router, w_in, w_out, layer):
    t, d = u.shape
    tm = MOE_TILE_ROWS
    n_tiles = (t * TOP_K) // tm + N_EXPERTS
    p = n_tiles * tm
    meta, counts = moe_router(u, w_router)
    cnt = counts[0, :N_EXPERTS].astype(jnp.int32)
    tiles_e = (cnt + tm - 1) // tm
    tile_end = jnp.cumsum(tiles_e)
    row_start = (tile_end - tiles_e) * tm
    tile_ids = jnp.arange(n_tiles, dtype=jnp.int32)
    tile_expert = jnp.minimum(jnp.searchsorted(tile_end, tile_ids, side='right'), N_EXPERTS - 1).astype(jnp.int32)
    tile_valid = (tile_ids < tile_end[-1]).astype(jnp.int32)
    tile_first = _first_flags(tile_expert)
    per = tm // MOE_OUT_ROWS
    sub_expert = jnp.repeat(tile_expert, per)
    sub_start = jnp.arange(n_tiles * per, dtype=jnp.int32) * MOE_OUT_ROWS
    sub_valid = (jnp.repeat(tile_valid, per) * (sub_start < (row_start + cnt)[sub_expert])).astype(jnp.int32)
    sub_first = _first_flags(sub_expert)
    e1 = meta[:, 0].astype(jnp.int32)
    e2 = meta[:, 1].astype(jnp.int32)
    pos1 = row_start[e1] + meta[:, 2].astype(jnp.int32)
    pos2 = row_start[e2] + meta[:, 3].astype(jnp.int32)
    tok = jnp.arange(t, dtype=jnp.int32)
    row_tok = jnp.zeros((p,), jnp.int32).at[pos1].set(tok).at[pos2].set(tok)
    xs = gather_rows(u, row_tok, BF16)
    h = expert_swiglu_in(xs, w_in, layer, tile_expert, tile_valid, tile_first)
    yb = expert_out(h, w_out, layer, sub_expert, sub_valid, sub_first)
    return combine_rows(yb, pos1, pos2, meta)


def kernel(x, c, ada_w, ada_b, ln_g, ln_b, attn_w_qkv, attn_w_o, rg_w_in, rg_conv_w, rg_conv_b,
           rg_gate_a_w, rg_gate_a_b, rg_gate_x_w, rg_gate_x_b, rg_lambda, rg_w_out,
           ffn_w_in, ffn_w_out, moe_w_router, moe_w_in, moe_w_out):
    b, s, d = x.shape
    depth = ada_w.shape[0]
    t = b * s
    alpha = (2 * depth) ** 0.25
    x = x.astype(F32)

    mod = ada_modulation(c, ada_w, ada_b)
    mod4 = [mod[i].reshape(b, N_MOD, 1, d) for i in range(depth)]
    SH1, SC1, G1, SH2, SC2, G2 = range(N_MOD)

    attn_dils = tuple(dl for _, dl in ATTN_GROUPS)
    us = modulate(x, mod4[0], SC1, SH1, BF16, dilations=attn_dils)
    for i in range(depth):
        j = i // 2
        is_attn = i % 2 == 0
        if is_attn:
            w_qkv = attn_w_qkv[j].astype(BF16)
            outs, lses = [], []
            for g, (window, dilation) in enumerate(ATTN_GROUPS):
                qkv_g = qkv_projection(us[g].reshape(t, d), w_qkv, g)
                o, lse = dilated_attention_group(qkv_g, g, window, dilation, b)
                outs.append(o)
                lses.append(lse)
            merged = merge_groups(outs, lses)
            y = matmul(merged, attn_w_o[j].astype(BF16), F32, 1024, 1024)
        else:
            zr = matmul(us[0].reshape(t, d), rg_w_in[j].astype(BF16), F32, 1024, 1024).reshape(b, s, -1)
            yr = rglru(zr, rg_conv_w[j], rg_conv_b[j], rg_gate_a_w[j], rg_gate_a_b[j],
                       rg_gate_x_w[j], rg_gate_x_b[j], rg_lambda[j])
            y = matmul(yr.reshape(t, d), rg_w_out[j].astype(BF16), F32, 1024, 1024)
        x, us = residual_layer_norm(x, y, mod4[i], G1, ln_g[i, 0], ln_b[i, 0], alpha,
                                    next_mod4=(mod4[i], SC2, SH2), u_dtype=BF16 if is_attn else F32)
        if is_attn:
            h = swiglu_in(us[0].reshape(t, d), ffn_w_in[j].astype(BF16), 1024)
            y = matmul(h, ffn_w_out[j].astype(BF16), F32, 512, 1024)
        else:
            y = moe_swiglu(us[0].reshape(t, d), moe_w_router[j], moe_w_in, moe_w_out, j)
        last = i + 1 == depth
        next_is_attn = (i + 1) % 2 == 0
        x, us = residual_layer_norm(x, y, mod4[i], G2, ln_g[i, 1], ln_b[i, 1], alpha,
                                    next_mod4=None if last else (mod4[i + 1], SC1, SH1), u_dtype=BF16,
                                    dilations=attn_dils if next_is_attn else (1,))
    return x
```
